```python
import functools
import jax, jax.numpy as jnp
from jax import lax
import numpy as np

D_MODEL = 1024
BATCH = 8
SEQ = 8192
DEPTH = 4
DEC_BATCH = 8
DEC_SEQ = 64
PAST_LEN = 4096

CHUNK = 64
EPS = 1e-6
LB_FLOOR = 1e-20
D_FF = 2816
N_MEM = 256
MEM_HEADS = 4
MEM_HEAD_DIM = D_MODEL // MEM_HEADS
A_HEADS = 4
A_DK = 64
A_DV = 64
A_KW = A_HEADS * A_DK
A_W = A_HEADS * A_DV
A_COLS = 2 * A_KW + 2 * A_W
B_HEADS = 6
B_DK = 32
B_DV = 64
B_KW = B_HEADS * B_DK
B_W = B_HEADS * B_DV
B_COLS = 2 * B_KW + 2 * B_W
ROPE_BASE = 10000.0
C_HEADS = 6
C_DH = 64
C_W = C_HEADS * C_DH
C_DECAY_RANK = 32
C_AAA_RANK = 32
C_GATE_RANK = 64
C_COLS = 3 * C_W + C_DECAY_RANK + C_AAA_RANK + C_GATE_RANK
RWKV_GN_EPS = 64e-5
MIX_W = A_W + B_W + C_W
IN_COLS = A_COLS + B_COLS + C_COLS

kernel_name = 'hymba_hgrn2_retnet_rwkv7_stream_step'


def rmsnorm(x, g):
    xf = x.astype(jnp.float32)
    y = xf * lax.rsqrt(jnp.mean(xf * xf, axis=-1, keepdims=True) + EPS)
    return (y * g.astype(jnp.float32)).astype(x.dtype)


def head_layernorm(x, eps):
    xf = x.astype(jnp.float32)
    mu = jnp.mean(xf, axis=-1, keepdims=True)
    var = jnp.mean(jnp.square(xf - mu), axis=-1, keepdims=True)
    return ((xf - mu) * lax.rsqrt(var + eps)).astype(x.dtype)


def swiglu(h, wg, wu, wd):
    return (jax.nn.silu(h @ wg) * (h @ wu)) @ wd


def rope(x, pos):
    half = x.shape[-1] // 2
    inv = ROPE_BASE ** (-jnp.arange(half, dtype=jnp.float32) / half)
    ang = pos.astype(jnp.float32)[:, None] * inv[None, :]
    cos = jnp.cos(ang)[None, :, None, :]
    sin = jnp.sin(ang)[None, :, None, :]
    x1 = x[..., :half].astype(jnp.float32)
    x2 = x[..., half:].astype(jnp.float32)
    return jnp.concatenate([x1 * cos - x2 * sin, x1 * sin + x2 * cos], axis=-1).astype(x.dtype)


def to_chunks(t, L):
    Bn, T, H, d = t.shape
    return t.reshape(Bn, T // L, L, H, d).transpose(1, 0, 3, 2, 4)


def from_chunks(t):
    N, Bn, H, L, d = t.shape
    return t.transpose(1, 0, 3, 2, 4).reshape(Bn, N * L, H, d)


def run_chunked(step, init, xs):
    T = xs[0].shape[1]
    L = min(T, CHUNK)
    S, o = lax.scan(step, init, tuple(to_chunks(t, L) for t in xs))
    return S, from_chunks(o)


def hgrn_chunk(S, inp):
    q, k, logf, v = inp
    L = q.shape[2]
    b = jnp.cumsum(logf, axis=2)
    causal = jnp.tril(jnp.ones((L, L), dtype=bool))[None, None, :, :, None]
    diff = b[:, :, :, None, :] - b[:, :, None, :, :]
    decay = jnp.where(causal, jnp.exp(jnp.minimum(diff, 0.0)), 0.0).astype(q.dtype)
    att = jnp.einsum('bhtk,bhsk,bhtsk->bhts', q, k, decay)
    o = jnp.einsum('bhts,bhsv->bhtv', att, v) + jnp.einsum('bhtk,bhkv->bhtv', q * jnp.exp(b), S)
    bL = b[:, :, -1:, :]
    S_new = jnp.exp(bL)[:, :, 0, :, None] * S + jnp.einsum('bhsk,bhsv->bhkv', k * jnp.exp(bL - b), v)
    return S_new, o


def retention_chunk(log_gamma, R, inp):
    q, k, v = inp
    L = q.shape[2]
    dt = q.dtype
    j = jnp.arange(L, dtype=jnp.float32)
    lg = log_gamma[:, None]
    rel = j[:, None] - j[None, :]
    D = jnp.where(rel >= 0, jnp.exp(lg[:, :, None] * jnp.maximum(rel, 0.0)), 0.0).astype(dt)
    inner = jnp.einsum('bhts,bhsv->bhtv', jnp.einsum('bhtk,bhsk->bhts', q, k) * D[None], v)
    cross = jnp.einsum('bhtk,bhkv->bhtv', q * jnp.exp(lg * (j + 1.0)).astype(dt)[None, :, :, None], R)
    v_w = v * jnp.exp(lg * (L - 1.0 - j)).astype(dt)[None, :, :, None]
    R_new = jnp.exp(lg * L).astype(dt)[None, :, :, None] * R + jnp.einsum('bhsk,bhsv->bhkv', k, v_w)
    return R_new, inner + cross


def rwkv7_scan(S0, r, w, k, v, kk, a):
    def step(S, inp):
        r_t, w_t, k_t, v_t, kk_t, a_t = inp
        sa = jnp.einsum('bhvk,bhk->bhv', S, -kk_t)
        S = S * w_t[:, :, None, :] + sa[..., None] * (kk_t * a_t)[:, :, None, :] + v_t[..., None] * k_t[:, :, None, :]
        return S, jnp.einsum('bhvk,bhk->bhv', S, r_t)
    xs = tuple(jnp.moveaxis(t, 1, 0) for t in (r, w, k, v, kk, a))
    S, ys = lax.scan(step, S0, xs)
    return S, jnp.moveaxis(ys, 0, 1)


def token_shift(p, prev, mu):
    p_prev = jnp.concatenate([prev.astype(p.dtype), p[:, :-1]], axis=1)
    return p + mu * (p_prev - p), p[:, -1:]


def mixers(h, pos, S_a, R_b, S_c, shift_c, p, log_gamma):
    Bn, T, _ = h.shape
    heads = lambda t, H: t.reshape(Bn, T, H, -1)
    proj = h @ p['w_in']
    pa = proj[..., :A_COLS]
    pb = proj[..., A_COLS:A_COLS + B_COLS]
    pc = proj[..., A_COLS + B_COLS:]
    qa, fa, ia, ga = jnp.split(pa, [A_KW, 2 * A_KW, 2 * A_KW + A_W], axis=-1)
    lb = p['hgrn_lb']
    log_lb = jnp.log(jnp.maximum(lb, LB_FLOOR))
    logf = jnp.logaddexp(log_lb, jnp.log1p(-lb) + jax.nn.log_sigmoid(fa.astype(jnp.float32))).astype(h.dtype)
    ka = -jnp.expm1(logf)
    qa = jax.nn.silu(qa) * (A_DK ** -0.5)
    S_a, oa = run_chunked(hgrn_chunk, S_a, (heads(qa, A_HEADS), heads(ka, A_HEADS), heads(logf, A_HEADS), heads(ia, A_HEADS)))
    oa = (rmsnorm(oa, p['hgrn_norm']) * jax.nn.silu(heads(ga, A_HEADS))).reshape(Bn, T, A_W)
    qb, kb, vb, gb = jnp.split(pb, [B_KW, 2 * B_KW, 2 * B_KW + B_W], axis=-1)
    qb = rope(heads(qb, B_HEADS), pos)
    kb = rope(heads(kb, B_HEADS), pos) * (B_DK ** -0.5)
    R_b, ob = run_chunked(functools.partial(retention_chunk, log_gamma), R_b, (qb, kb, heads(vb, B_HEADS)))
    ob = head_layernorm(ob, EPS).reshape(Bn, T, B_W) * jax.nn.silu(gb)
    pc, shift_new = token_shift(pc, shift_c, p['rwkv_mu'])
    rc, kc, vc, wd, ad, gd = jnp.split(pc, [C_W, 2 * C_W, 3 * C_W, 3 * C_W + C_DECAY_RANK, 3 * C_W + C_DECAY_RANK + C_AAA_RANK], axis=-1)
    w = -jax.nn.softplus(-(p['rwkv_w0'] + jnp.tanh(wd) @ p['rwkv_w2'])) - 0.5
    decay = jnp.exp(-jnp.exp(w))
    a = jax.nn.sigmoid(p['rwkv_a0'] + ad @ p['rwkv_a2'])
    g = jax.nn.sigmoid(gd) @ p['rwkv_g2']
    kk = heads(kc * p['rwkv_k_k'], C_HEADS)
    kk = kk / jnp.maximum(jnp.sqrt(jnp.sum(kk * kk, axis=-1, keepdims=True)), 1e-6)
    kc = kc * (1.0 + (a - 1.0) * p['rwkv_k_a'])
    rh, kh, vh = heads(rc, C_HEADS), heads(kc, C_HEADS), heads(vc, C_HEADS)
    S_c, yc = rwkv7_scan(S_c, rh, heads(decay, C_HEADS), kh, vh, kk, heads(a, C_HEADS))
    yc = head_layernorm(yc, RWKV_GN_EPS).reshape(Bn, T, C_W) * p['rwkv_ln_w'] + p['rwkv_ln_b']
    bonus = (jnp.sum(rh * kh * p['rwkv_r_k'], axis=-1, keepdims=True) * vh).reshape(Bn, T, C_W)
    yc = (yc + bonus) * g
    out = jnp.concatenate([oa, ob, yc], axis=-1) @ p['w_out']
    return out, S_a, R_b, S_c, shift_new


def mem_kv(mem, g, wk, wv):
    Bn, M, _ = mem.shape
    hm = rmsnorm(mem, g)
    return (hm @ wk).reshape(Bn, M, MEM_HEADS, MEM_HEAD_DIM), (hm @ wv).reshape(Bn, M, MEM_HEADS, MEM_HEAD_DIM)


def cross_attn(h, mk, mv, wq, wo):
    Bn, T, _ = h.shape
    q = (h @ wq).reshape(Bn, T, MEM_HEADS, MEM_HEAD_DIM)
    s = jnp.einsum('bthd,bmhd->bhtm', q, mk).astype(jnp.float32) * (MEM_HEAD_DIM ** -0.5)
    pr = jax.nn.softmax(s, axis=-1).astype(h.dtype)
    o = jnp.einsum('bhtm,bmhd->bthd', pr, mv).reshape(Bn, T, D_MODEL)
    return o @ wo


MIXER_KEYS = ('w_in', 'hgrn_norm', 'rwkv_mu', 'rwkv_w0', 'rwkv_w2', 'rwkv_a0', 'rwkv_a2', 'rwkv_g2',
              'rwkv_k_k', 'rwkv_k_a', 'rwkv_r_k', 'rwkv_ln_w', 'rwkv_ln_b', 'w_out')


def trunk(x, pos, mem_k, mem_v, st_a, st_b, st_c, st_sh, W, lbs, log_gamma):
    new_a, new_b, new_c, new_sh = [], [], [], []
    for l in range(DEPTH):
        x = x + 0.5 * swiglu(rmsnorm(x, W['norm_ffn1'][l]), W['ffn1_wg'][l], W['ffn1_wu'][l], W['ffn1_wd'][l])
        p = {name: W[name][l] for name in MIXER_KEYS}
        p['hgrn_lb'] = lbs[l]
        m, sa, sb, sc, sh = mixers(rmsnorm(x, W['norm_mix'][l]), pos, st_a[l], st_b[l], st_c[l], st_sh[l], p, log_gamma)
        x = x + m
        x = x + cross_attn(rmsnorm(x, W['norm_cross'][l]), mem_k[l], mem_v[l], W['wq_x'][l], W['wo_x'][l])
        x = x + 0.5 * swiglu(rmsnorm(x, W['norm_ffn2'][l]), W['ffn2_wg'][l], W['ffn2_wu'][l], W['ffn2_wd'][l])
        new_a.append(sa)
        new_b.append(sb)
        new_c.append(sc)
        new_sh.append(sh)
    return rmsnorm(x, W['final_norm']), jnp.stack(new_a), jnp.stack(new_b), jnp.stack(new_c), jnp.stack(new_sh)


def setup_inputs(seed: int = 0) -> dict:
    key = jax.random.key(seed)
    ks = iter(jax.random.split(key, 64))
    def nrm(shape, scale):
        return jax.random.normal(next(ks), shape, jnp.float32) * scale
    def gain(shape):
        return 1.0 + nrm(shape, 0.02)
    d_in = D_MODEL ** -0.5
    return {
        'x_prompt': nrm((BATCH, SEQ, D_MODEL), 1.0),
        'x_sample': nrm((DEC_BATCH, DEC_SEQ, D_MODEL), 1.0),
        'mem_prompt': nrm((BATCH, N_MEM, D_MODEL), 1.0),
        'cache_mem_k': nrm((DEPTH, DEC_BATCH, N_MEM, MEM_HEADS, MEM_HEAD_DIM), 1.0),
        'cache_mem_v': nrm((DEPTH, DEC_BATCH, N_MEM, MEM_HEADS, MEM_HEAD_DIM), 1.0),
        'state_hgrn': nrm((DEPTH, DEC_BATCH, A_HEADS, A_DK, A_DV), 0.5),
        'state_ret': nrm((DEPTH, DEC_BATCH, B_HEADS, B_DK, B_DV), 1.0),
        'state_rwkv': nrm((DEPTH, DEC_BATCH, C_HEADS, C_DH, C_DH), 0.5),
        'state_rwkv_shift': nrm((DEPTH, DEC_BATCH, 1, C_COLS), 1.0),
        'norm_ffn1': gain((DEPTH, D_MODEL)),
        'ffn1_wg': nrm((DEPTH, D_MODEL, D_FF), d_in),
        'ffn1_wu': nrm((DEPTH, D_MODEL, D_FF), d_in),
        'ffn1_wd': nrm((DEPTH, D_FF, D_MODEL), D_FF ** -0.5),
        'norm_mix': gain((DEPTH, D_MODEL)),
        'w_in': nrm((DEPTH, D_MODEL, IN_COLS), d_in),
        'hgrn_lb_param': nrm((DEPTH, A_KW), 0.5),
        'hgrn_norm': gain((DEPTH, A_DV)),
        'rwkv_mu': jax.random.uniform(next(ks), (DEPTH, C_COLS), jnp.float32, 0.1, 0.9),
        'rwkv_w0': nrm((DEPTH, C_W), 0.5) - 2.0,
        'rwkv_w2': nrm((DEPTH, C_DECAY_RANK, C_W), 0.1),
        'rwkv_a0': nrm((DEPTH, C_W), 0.3),
        'rwkv_a2': nrm((DEPTH, C_AAA_RANK, C_W), 0.1),
        'rwkv_g2': nrm((DEPTH, C_GATE_RANK, C_W), C_GATE_RANK ** -0.5),
        'rwkv_k_k': 0.85 + nrm((DEPTH, C_W), 0.05),
        'rwkv_k_a': 1.0 + nrm((DEPTH, C_W), 0.05),
        'rwkv_r_k': nrm((DEPTH, C_HEADS, C_DH), 0.1),
        'rwkv_ln_w': gain((DEPTH, C_W)),
        'rwkv_ln_b': nrm((DEPTH, C_W), 0.01),
        'w_out': nrm((DEPTH, MIX_W, D_MODEL), MIX_W ** -0.5),
        'norm_cross': gain((DEPTH, D_MODEL)),
        'norm_mem': gain((DEPTH, D_MODEL)),
        'wq_x': nrm((DEPTH, D_MODEL, D_MODEL), d_in),
        'wk_x': nrm((DEPTH, D_MODEL, D_MODEL), d_in),
        'wv_x': nrm((DEPTH, D_MODEL, D_MODEL), d_in),
        'wo_x': nrm((DEPTH, D_MODEL, D_MODEL), d_in),
        'norm_ffn2': gain((DEPTH, D_MODEL)),
        'ffn2_wg': nrm((DEPTH, D_MODEL, D_FF), d_in),
        'ffn2_wu': nrm((DEPTH, D_MODEL, D_FF), d_in),
        'ffn2_wd': nrm((DEPTH, D_FF, D_MODEL), D_FF ** -0.5),
        'final_norm': gain((D_MODEL,)),
    }


def reference(x_prompt, x_sample, mem_prompt, cache_mem_k, cache_mem_v, state_hgrn, state_ret, state_rwkv,
              state_rwkv_shift, norm_ffn1, ffn1_wg, ffn1_wu, ffn1_wd, norm_mix, w_in, hgrn_lb_param, hgrn_norm,
              rwkv_mu, rwkv_w0, rwkv_w2, rwkv_a0, rwkv_a2, rwkv_g2, rwkv_k_k, rwkv_k_a, rwkv_r_k, rwkv_ln_w,
              rwkv_ln_b, w_out, norm_cross, norm_mem, wq_x, wk_x, wv_x, wo_x, norm_ffn2, ffn2_wg, ffn2_wu,
              ffn2_wd, final_norm):
    W = dict(norm_ffn1=norm_ffn1, ffn1_wg=ffn1_wg, ffn1_wu=ffn1_wu, ffn1_wd=ffn1_wd, norm_mix=norm_mix,
             w_in=w_in, hgrn_norm=hgrn_norm, rwkv_mu=rwkv_mu, rwkv_w0=rwkv_w0, rwkv_w2=rwkv_w2,
             rwkv_a0=rwkv_a0, rwkv_a2=rwkv_a2, rwkv_g2=rwkv_g2, rwkv_k_k=rwkv_k_k, rwkv_k_a=rwkv_k_a,
             rwkv_r_k=rwkv_r_k, rwkv_ln_w=rwkv_ln_w, rwkv_ln_b=rwkv_ln_b, w_out=w_out, norm_cross=norm_cross,
             wq_x=wq_x, wo_x=wo_x, norm_ffn2=norm_ffn2, ffn2_wg=ffn2_wg, ffn2_wu=ffn2_wu, ffn2_wd=ffn2_wd,
             final_norm=final_norm)
    sm = jax.nn.softmax(hgrn_lb_param.astype(jnp.float32), axis=0)
    lbs = jnp.cumsum(sm, axis=0) - sm[0:1]
    log_gamma = jnp.log1p(-jnp.exp2(-5.0 - jnp.arange(B_HEADS, dtype=jnp.float32)))

    Bp, Tp, _ = x_prompt.shape
    dt = x_prompt.dtype
    pos_p = jnp.arange(Tp, dtype=jnp.int32)
    mks, mvs = [], []
    for l in range(DEPTH):
        mk, mv = mem_kv(mem_prompt, norm_mem[l], wk_x[l], wv_x[l])
        mks.append(mk)
        mvs.append(mv)
    p_mem_k = jnp.stack(mks)
    p_mem_v = jnp.stack(mvs)
    y_prompt, p_hgrn, p_ret, p_rwkv, p_shift = trunk(
        x_prompt, pos_p, p_mem_k, p_mem_v,
        jnp.zeros((DEPTH, Bp, A_HEADS, A_DK, A_DV), dt),
        jnp.zeros((DEPTH, Bp, B_HEADS, B_DK, B_DV), dt),
        jnp.zeros((DEPTH, Bp, C_HEADS, C_DH, C_DH), dt),
        jnp.zeros((DEPTH, Bp, 1, C_COLS), dt),
        W, lbs, log_gamma)

    Ts = x_sample.shape[1]
    pos_s = PAST_LEN + jnp.arange(Ts, dtype=jnp.int32)
    y_sample, s_hgrn, s_ret, s_rwkv, s_shift = trunk(
        x_sample, pos_s, cache_mem_k, cache_mem_v, state_hgrn, state_ret, state_rwkv, state_rwkv_shift,
        W, lbs, log_gamma)

    return (y_prompt, y_sample, p_hgrn, p_ret, p_rwkv, p_shift, p_mem_k, p_mem_v, s_hgrn, s_ret, s_rwkv, s_shift)
```

```python
import functools

import jax
import jax.numpy as jnp
from jax import lax
from jax.experimental import pallas as pl
from jax.experimental.pallas import tpu as pltpu

F32 = jnp.float32
BF16 = jnp.bfloat16

EPS = 1e-6
LB_FLOOR = 1e-20
ROPE_BASE = 10000.0
PAST_LEN = 4096
MEM_HEADS = 4
A_HEADS, A_DK, A_DV = 4, 64, 64
A_KW = A_HEADS * A_DK
A_W = A_HEADS * A_DV
A_COLS = 2 * A_KW + 2 * A_W
B_HEADS, B_DK, B_DV = 6, 32, 64
B_KW = B_HEADS * B_DK
B_W = B_HEADS * B_DV
B_COLS = 2 * B_KW + 2 * B_W
C_HEADS, C_DH = 6, 64
C_W = C_HEADS * C_DH
C_DECAY_RANK, C_AAA_RANK, C_GATE_RANK = 32, 32, 64
C_LORA = C_DECAY_RANK + C_AAA_RANK + C_GATE_RANK
C_COLS = 3 * C_W + C_LORA
RWKV_GN_EPS = 64e-5

LANES = 128
VMEM_LIMIT = 56 * 1024 * 1024
FFN_TILE = 512
FFN_FCHUNK = 256
TOK_TILE = 512
HGRN_BLOCK = 16
HGRN_TILE = 128
RET_TILE = 256
RWKV_CHUNK = 64


def _cparams(*sem):
    return pltpu.CompilerParams(dimension_semantics=sem, vmem_limit_bytes=VMEM_LIMIT)


def _mm(a, b):
    return jnp.dot(a.astype(BF16), b.astype(BF16), preferred_element_type=F32)


def _mm_nt(a, b):
    return lax.dot_general(a.astype(BF16), b.astype(BF16), (((1,), (1,)), ((), ())),
                           preferred_element_type=F32)


def _mm_tn(a, b):
    return lax.dot_general(a.astype(BF16), b.astype(BF16), (((0,), (0,)), ((), ())),
                           preferred_element_type=F32)


def _split3(x):
    hi = x.astype(BF16)
    r = x - hi.astype(F32)
    mid = r.astype(BF16)
    lo = (r - mid.astype(F32)).astype(BF16)
    return hi, mid, lo


def _mm_exact_lhs(c, x):
    hi, mid, lo = _split3(x)
    return (jnp.dot(c, hi, preferred_element_type=F32) + jnp.dot(c, mid, preferred_element_type=F32)
            + jnp.dot(c, lo, preferred_element_type=F32))


def _mm_exact_rhs(x, c):
    hi, mid, lo = _split3(x)
    return (jnp.dot(hi, c, preferred_element_type=F32) + jnp.dot(mid, c, preferred_element_type=F32)
            + jnp.dot(lo, c, preferred_element_type=F32))


def _rms(x, g):
    return x * lax.rsqrt(jnp.mean(x * x, axis=-1, keepdims=True) + EPS) * g


def _sigmoid(x):
    return 1.0 / (1.0 + jnp.exp(-x))


def _silu(x):
    return x * _sigmoid(x)


def _softplus(x):
    return jnp.maximum(x, 0.0) + jnp.log1p(jnp.exp(-jnp.abs(x)))


def _logaddexp(a, b):
    return jnp.maximum(a, b) + jnp.log1p(jnp.exp(-jnp.abs(a - b)))


def _iota2(shape, axis):
    return lax.broadcasted_iota(jnp.int32, shape, axis)


def _tri_incl(n):
    return (_iota2((n, n), 0) >= _iota2((n, n), 1)).astype(BF16)


def _block_ones(n, blk):
    return (_iota2((n, n), 0) // blk) == (_iota2((n, n), 1) // blk)


def _seg_sum(x, blk):
    n = x.shape[-1]
    outs = []
    bo = _block_ones(LANES, blk).astype(BF16)
    for p in range(n // LANES):
        outs.append(_mm_exact_rhs(x[:, p * LANES:(p + 1) * LANES], bo))
    return outs[0] if len(outs) == 1 else jnp.concatenate(outs, axis=-1)


def _ffn_body(x_ref, g_ref, wg_ref, wu_ref, wd_ref, gf_ref, o_ref, *, apply_final_norm):
    x = x_ref[...]
    hb = _rms(x, g_ref[...]).astype(BF16)
    d_ff = wg_ref.shape[1]
    acc = jnp.zeros(x.shape, F32)
    for j in range(d_ff // FFN_FCHUNK):
        cols = slice(j * FFN_FCHUNK, (j + 1) * FFN_FCHUNK)
        a = jnp.dot(hb, wg_ref[:, cols], preferred_element_type=F32)
        u = jnp.dot(hb, wu_ref[:, cols], preferred_element_type=F32)
        t = (_silu(a) * u).astype(BF16)
        acc = acc + jnp.dot(t, wd_ref[cols, :], preferred_element_type=F32)
    y = x + 0.5 * acc
    if apply_final_norm:
        y = _rms(y, gf_ref[...])
    o_ref[...] = y


def _ffn(x, g, wg, wu, wd, gf, apply_final_norm):
    n, d = x.shape
    d_ff = wg.shape[1]
    tm = min(FFN_TILE, n)
    const = lambda i: (0, 0)
    return pl.pallas_call(
        functools.partial(_ffn_body, apply_final_norm=apply_final_norm),
        grid=(n // tm,),
        in_specs=[pl.BlockSpec((tm, d), lambda i: (i, 0)),
                  pl.BlockSpec((1, d), const),
                  pl.BlockSpec((d, d_ff), const),
                  pl.BlockSpec((d, d_ff), const),
                  pl.BlockSpec((d_ff, d), const),
                  pl.BlockSpec((1, d), const)],
        out_specs=pl.BlockSpec((tm, d), lambda i: (i, 0)),
        out_shape=jax.ShapeDtypeStruct((n, d), F32),
        compiler_params=_cparams("parallel"),
        name="ffn",
    )(x, g, wg, wu, wd, gf)


def _inproj_body(x_ref, g_ref, wa_ref, wb_ref, wc_ref, pa_ref, pb_ref, pc_ref):
    hb = _rms(x_ref[...], g_ref[...]).astype(BF16)
    pa_ref[...] = jnp.dot(hb, wa_ref[...], preferred_element_type=F32)
    pb_ref[...] = jnp.dot(hb, wb_ref[...], preferred_element_type=F32)
    pc_ref[...] = jnp.dot(hb, wc_ref[...], preferred_element_type=F32)


def _inproj(x, g, wa, wb, wc):
    n, d = x.shape
    tm = min(TOK_TILE, n)
    const = lambda i: (0, 0)
    row = lambda i: (i, 0)
    widths = (wa.shape[1], wb.shape[1], wc.shape[1])
    return pl.pallas_call(
        _inproj_body,
        grid=(n // tm,),
        in_specs=[pl.BlockSpec((tm, d), row), pl.BlockSpec((1, d), const)]
                 + [pl.BlockSpec((d, w), const) for w in widths],
        out_specs=[pl.BlockSpec((tm, w), row) for w in widths],
        out_shape=[jax.ShapeDtypeStruct((n, w), F32) for w in widths],
        compiler_params=_cparams("parallel"),
        name="inproj",
    )(x, g, wa, wb, wc)


def _memkv_body(m_ref, g_ref, wk_ref, wv_ref, k_ref, v_ref):
    hb = _rms(m_ref[...], g_ref[...]).astype(BF16)
    k_ref[...] = jnp.dot(hb, wk_ref[...], preferred_element_type=F32)
    v_ref[...] = jnp.dot(hb, wv_ref[...], preferred_element_type=F32)


def _memkv(mem, g, wk, wv):
    n, d = mem.shape
    tm = min(TOK_TILE, n)
    const = lambda i: (0, 0)
    row = lambda i: (i, 0)
    return pl.pallas_call(
        _memkv_body,
        grid=(n // tm,),
        in_specs=[pl.BlockSpec((tm, d), row), pl.BlockSpec((1, d), const),
                  pl.BlockSpec((d, d), const), pl.BlockSpec((d, d), const)],
        out_specs=[pl.BlockSpec((tm, d), row), pl.BlockSpec((tm, d), row)],
        out_shape=[jax.ShapeDtypeStruct((n, d), F32)] * 2,
        compiler_params=_cparams("parallel"),
        name="memkv",
    )(mem, g, wk, wv)


def _hgrn_body(pa_ref, loglb_ref, log1mlb_ref, gn_ref, m0_ref, oa_ref, mout_ref, m_scr, *, nblk):
    c = pl.program_id(1)
    blk = HGRN_BLOCK

    @pl.when(c == 0)
    def _():
        m_scr[...] = m0_ref[0]

    log_lb = loglb_ref[...]
    log1m_lb = log1mlb_ref[...]
    gn = gn_ref[...]
    tri = _tri_incl(blk)
    head_ones = _block_ones(A_KW, A_DK)
    head_ones_b = head_ones.astype(BF16)
    head_mean_b = head_ones_b
    row_id = _iota2((blk, A_KW), 0)

    def block(j, carry):
        rows = pl.ds(pl.multiple_of(j * blk, blk), blk)
        q = pa_ref[0, rows, 0:A_KW]
        f = pa_ref[0, rows, A_KW:2 * A_KW]
        v = pa_ref[0, rows, 2 * A_KW:2 * A_KW + A_W]
        g = pa_ref[0, rows, 2 * A_KW + A_W:A_COLS]
        logf = _logaddexp(log_lb, log1m_lb - _softplus(-f))
        ka = 1.0 - jnp.exp(logf)
        qa = _silu(q) * (A_DK ** -0.5)
        b = _mm_exact_lhs(tri, logf)
        b_last = b[blk - 1:blk, :]
        m = m_scr[...]
        o = _mm_nt(qa * jnp.exp(b), m)
        slabs = []
        for s in range(blk):
            z = qa * ka[s:s + 1, :] * jnp.exp(jnp.minimum(b - b[s:s + 1, :], 0.0))
            slabs.append(jnp.where(row_id >= s, z, 0.0))
        att = _mm(jnp.concatenate(slabs, axis=0), head_ones_b)
        for s in range(blk):
            o = o + att[s * blk:(s + 1) * blk, :] * v[s:s + 1, :]
        ms = _mm_exact_rhs(o * o, head_mean_b) * (1.0 / A_DV)
        oa_ref[0, rows, :] = o * lax.rsqrt(ms + EPS) * gn * _silu(g)
        k_hat = ka * jnp.exp(b_last - b)
        m_scr[...] = m * jnp.exp(b_last) + jnp.where(head_ones, _mm_tn(v, k_hat), 0.0)
        return carry

    lax.fori_loop(0, nblk, block, 0)

    @pl.when(c == pl.num_programs(1) - 1)
    def _():
        mout_ref[0] = m_scr[...]


def _hgrn(pa, log_lb, log1m_lb, gn, m0):
    bsz, t, _ = pa.shape
    tile = min(HGRN_TILE, t)
    const = lambda b, c: (0, 0)
    return pl.pallas_call(
        functools.partial(_hgrn_body, nblk=tile // HGRN_BLOCK),
        grid=(bsz, t // tile),
        in_specs=[pl.BlockSpec((1, tile, A_COLS), lambda b, c: (b, c, 0)),
                  pl.BlockSpec((1, A_KW), const), pl.BlockSpec((1, A_KW), const),
                  pl.BlockSpec((1, A_W), const),
                  pl.BlockSpec((1, A_W, A_KW), lambda b, c: (b, 0, 0))],
        out_specs=[pl.BlockSpec((1, tile, A_W), lambda b, c: (b, c, 0)),
                   pl.BlockSpec((1, A_W, A_KW), lambda b, c: (b, 0, 0))],
        out_shape=[jax.ShapeDtypeStruct((bsz, t, A_W), F32),
                   jax.ShapeDtypeStruct((bsz, A_W, A_KW), F32)],
        scratch_shapes=[pltpu.VMEM((A_W, A_KW), F32)],
        compiler_params=_cparams("parallel", "arbitrary"),
        name="hgrn",
    )(pa, log_lb, log1m_lb, gn, m0)


def _ret_body(lgs_ref, pb_ref, cos_ref, sin_ref, swap_ref, lg_ref, m0_ref, ob_ref, mout_ref, m_scr, *,
              tile):
    c = pl.program_id(1)

    @pl.when(c == 0)
    def _():
        m_scr[...] = m0_ref[0]

    x = pb_ref[0]
    qk = x[:, 0:2 * B_KW]
    v = x[:, 2 * B_KW:2 * B_KW + B_W]
    g = x[:, 2 * B_KW + B_W:B_COLS]
    qk = qk * cos_ref[...] + _mm_exact_rhs(qk, swap_ref[...]) * sin_ref[...]
    q = qk[:, 0:B_KW]
    k = qk[:, B_KW:2 * B_KW] * (B_DK ** -0.5)

    lg_k = lg_ref[0:1, 0:B_KW]
    lg_v = lg_ref[1:2, :]
    j_k = _iota2((tile, B_KW), 0).astype(F32)
    j_v = _iota2((tile, B_W), 0).astype(F32)
    m = m_scr[...]
    cross = _mm_nt(q * jnp.exp(lg_k * (j_k + 1.0)), m)
    rel = (_iota2((tile, tile), 0) - _iota2((tile, tile), 1)).astype(F32)
    outs = []
    for h in range(B_HEADS):
        lg_h = lgs_ref[h]
        dmat = jnp.where(rel >= 0.0, jnp.exp(lg_h * jnp.maximum(rel, 0.0)), 0.0)
        s = _mm_nt(q[:, h * B_DK:(h + 1) * B_DK], k[:, h * B_DK:(h + 1) * B_DK]) * dmat
        outs.append(_mm(s, v[:, h * B_DV:(h + 1) * B_DV]))
    o = jnp.concatenate(outs, axis=-1) + cross
    mu = _seg_sum(o, B_DV) * (1.0 / B_DV)
    oc = o - mu
    var = _seg_sum(oc * oc, B_DV) * (1.0 / B_DV)
    ob_ref[0] = oc * lax.rsqrt(var + EPS) * _silu(g)
    v_w = v * jnp.exp(lg_v * (tile - 1.0 - j_v))
    head_mask = (_iota2((B_W, B_KW), 0) // B_DV) == (_iota2((B_W, B_KW), 1) // B_DK)
    m_scr[...] = m * jnp.exp(lg_k * float(tile)) + jnp.where(head_mask, _mm_tn(v_w, k), 0.0)

    @pl.when(c == pl.num_programs(1) - 1)
    def _():
        mout_ref[0] = m_scr[...]


def _retention(log_gamma, pb, cos_t, sin_t, swap, lg_rows, m0):
    bsz, t, _ = pb.shape
    tile = min(RET_TILE, t)
    const = lambda b, c: (0, 0)
    return pl.pallas_call(
        functools.partial(_ret_body, tile=tile),
        grid=(bsz, t // tile),
        in_specs=[pl.BlockSpec(memory_space=pltpu.SMEM),
                  pl.BlockSpec((1, tile, B_COLS), lambda b, c: (b, c, 0)),
                  pl.BlockSpec((tile, 2 * B_KW), lambda b, c: (c, 0)),
                  pl.BlockSpec((tile, 2 * B_KW), lambda b, c: (c, 0)),
                  pl.BlockSpec((2 * B_KW, 2 * B_KW), const),
                  pl.BlockSpec((2, B_W), const),
                  pl.BlockSpec((1, B_W, B_KW), lambda b, c: (b, 0, 0))],
        out_specs=[pl.BlockSpec((1, tile, B_W), lambda b, c: (b, c, 0)),
                   pl.BlockSpec((1, B_W, B_KW), lambda b, c: (b, 0, 0))],
        out_shape=[jax.ShapeDtypeStruct((bsz, t, B_W), F32),
                   jax.ShapeDtypeStruct((bsz, B_W, B_KW), F32)],
        scratch_shapes=[pltpu.VMEM((B_W, B_KW), F32)],
        compiler_params=_cparams("parallel", "arbitrary"),
        name="retention",
    )(log_gamma, pb, cos_t, sin_t, swap, lg_rows, m0)


def _rwkv_body(pc_ref, sh0_ref, s0_ref, mu_ref, vec_ref, w2_ref, a2_ref, g2_ref,
               yc_ref, sout_ref, shout_ref, s_scr, sh_scr):
    c = pl.program_id(1)
    L = RWKV_CHUNK
    npair = C_HEADS // 2

    @pl.when(c == 0)
    def _():
        s_scr[...] = s0_ref[0]
        sh_scr[...] = sh0_ref[0]

    p = pc_ref[0]
    row = _iota2((L, C_COLS), 0)
    prev = jnp.where(row == 0, sh_scr[...], pltpu.roll(p, 1, 0))
    xs = p + mu_ref[...] * (prev - p)
    sh_scr[...] = p[L - 1:L, :]

    r = xs[:, 0:C_W]
    k = xs[:, C_W:2 * C_W]
    v = xs[:, 2 * C_W:3 * C_W]
    lora = xs[:, 3 * C_W:C_COLS]
    w0, a0, k_k, k_a, r_k, ln_w, ln_b = (vec_ref[i:i + 1, :] for i in range(7))
    w = -_softplus(-(w0 + _mm(jnp.tanh(lora), w2_ref[...]))) - 0.5
    lw = -jnp.exp(w)
    a = _sigmoid(a0 + _mm(lora, a2_ref[...]))
    g = _mm(_sigmoid(lora), g2_ref[...])
    kk = k * k_k
    kk = kk / jnp.maximum(jnp.sqrt(_seg_sum(kk * kk, C_DH)), 1e-6)
    kc = k * (1.0 + (a - 1.0) * k_a)
    beta = kk * a

    b = _mm_exact_lhs(_tri_incl(L), lw)
    b_prev = b - lw
    b_last = b[L - 1:L, :]
    e_b = jnp.exp(b)
    e_nb = jnp.exp(-b)
    e_tail = jnp.exp(b_last - b)
    kap_t = kk * jnp.exp(b_prev)
    r_t = r * e_b
    beta_t = beta * e_nb
    k_t = kc * e_nb
    beta_h = beta * e_tail
    k_h = kc * e_tail
    p_last = jnp.exp(b_last)

    ti = _iota2((L, L), 0)
    si = _iota2((L, L), 1)
    strict = ti > si
    incl = ti >= si
    eye = (ti == si).astype(F32)
    lane = _iota2((L, LANES), 1)
    pair_diag = _block_ones(LANES, C_DH)

    ys = []
    for pr in range(npair):
        cols = slice(pr * LANES, (pr + 1) * LANES)
        s_mat = s_scr[pr]
        kap_p, r_p, beta_p, k_p = kap_t[:, cols], r_t[:, cols], beta_t[:, cols], k_t[:, cols]
        v_p = v[:, cols]
        kh0 = _mm_nt(kap_p, s_mat)
        y_p = _mm_nt(r_p, s_mat)
        bk = jnp.concatenate([beta_p, k_p], axis=0)
        u_p = jnp.zeros((L, LANES), F32)
        cu_ev = []
        for hh in range(2):
            hm = (lane // C_DH) == hh
            lhs = jnp.concatenate([jnp.where(hm, kap_p, 0.0), jnp.where(hm, r_p, 0.0)], axis=0)
            gram = _mm_nt(lhs, bk)
            a_m = jnp.where(strict, gram[0:L, 0:L], 0.0)
            b_m = jnp.where(strict, gram[0:L, L:2 * L], 0.0)
            c_m = jnp.where(incl, gram[L:2 * L, 0:L], 0.0)
            e_m = jnp.where(incl, gram[L:2 * L, L:2 * L], 0.0)
            n_pow = -a_m
            t_inv = eye + n_pow
            steps = L.bit_length() - 2
            for it in range(steps):
                n_pow = _mm(n_pow, n_pow)
                t_inv = t_inv + _mm(n_pow, t_inv)
            rhs = -kh0 - _mm(b_m, v_p)
            u_h = jnp.where(hm, _mm(t_inv, rhs), 0.0)
            u_p = u_p + u_h
            cu_ev.append((c_m, e_m, hm))
        uv = jnp.concatenate([u_p, v_p], axis=0)
        for c_m, e_m, hm in cu_ev:
            y_p = y_p + jnp.where(hm, _mm(jnp.concatenate([c_m, e_m], axis=1), uv), 0.0)
        ys.append(y_p)
        bkh = jnp.concatenate([beta_h[:, cols], k_h[:, cols]], axis=0)
        s_scr[pr] = s_mat * p_last[:, cols] + jnp.where(pair_diag, _mm_tn(uv, bkh), 0.0)

    y = jnp.concatenate(ys, axis=-1)
    mu_y = _seg_sum(y, C_DH) * (1.0 / C_DH)
    yc = y - mu_y
    var = _seg_sum(yc * yc, C_DH) * (1.0 / C_DH)
    yn = yc * lax.rsqrt(var + RWKV_GN_EPS) * ln_w + ln_b
    bonus = _seg_sum(r * kc * r_k, C_DH) * v
    yc_ref[0] = (yn + bonus) * g

    @pl.when(c == pl.num_programs(1) - 1)
    def _():
        sout_ref[0] = s_scr[...]
        shout_ref[0] = sh_scr[...]


def _rwkv(pc, sh0, s0, mu, vecs, w2p, a2p, g2p):
    bsz, t, _ = pc.shape
    L = RWKV_CHUNK
    npair = C_HEADS // 2
    const = lambda b, c: (0, 0)
    return pl.pallas_call(
        _rwkv_body,
        grid=(bsz, t // L),
        in_specs=[pl.BlockSpec((1, L, C_COLS), lambda b, c: (b, c, 0)),
                  pl.BlockSpec((1, 1, C_COLS), lambda b, c: (b, 0, 0)),
                  pl.BlockSpec((1, npair, LANES, LANES), lambda b, c: (b, 0, 0, 0)),
                  pl.BlockSpec((1, C_COLS), const),
                  pl.BlockSpec((8, C_W), const),
                  pl.BlockSpec((C_LORA, C_W), const),
                  pl.BlockSpec((C_LORA, C_W), const),
                  pl.BlockSpec((C_LORA, C_W), const)],
        out_specs=[pl.BlockSpec((1, L, C_W), lambda b, c: (b, c, 0)),
                   pl.BlockSpec((1, npair, LANES, LANES), lambda b, c: (b, 0, 0, 0)),
                   pl.BlockSpec((1, 1, C_COLS), lambda b, c: (b, 0, 0))],
        out_shape=[jax.ShapeDtypeStruct((bsz, t, C_W), F32),
                   jax.ShapeDtypeStruct((bsz, npair, LANES, LANES), F32),
                   jax.ShapeDtypeStruct((bsz, 1, C_COLS), F32)],
        scratch_shapes=[pltpu.VMEM((npair, LANES, LANES), F32), pltpu.VMEM((1, C_COLS), F32)],
        compiler_params=_cparams("parallel", "arbitrary"),
        name="rwkv",
    )(pc, sh0, s0, mu, vecs, w2p, a2p, g2p)


def _outattn_body(x_ref, oa_ref, ob_ref, yc_ref, woa_ref, wob_ref, woc_ref, g_ref, wq_ref, wo_ref,
                  mk_ref, mv_ref, o_ref):
    x = (x_ref[0] + _mm(oa_ref[0], woa_ref[...]) + _mm(ob_ref[0], wob_ref[...])
         + _mm(yc_ref[0], woc_ref[...]))
    q = _mm(_rms(x, g_ref[...]), wq_ref[...])
    d = x.shape[-1]
    hd = d // MEM_HEADS
    outs = []
    for h in range(MEM_HEADS):
        cols = slice(h * hd, (h + 1) * hd)
        s = _mm_nt(q[:, cols], mk_ref[0, :, cols]) * (hd ** -0.5)
        s = s - jnp.max(s, axis=-1, keepdims=True)
        e = jnp.exp(s)
        pr = e / jnp.sum(e, axis=-1, keepdims=True)
        outs.append(_mm(pr, mv_ref[0, :, cols]))
    o_ref[0] = x + _mm(jnp.concatenate(outs, axis=-1), wo_ref[...])


def _outattn(x, oa, ob, yc, woa, wob, woc, g, wq, wo, mk, mv):
    bsz, t, d = x.shape
    tm = min(TOK_TILE, t)
    n_mem = mk.shape[1]
    const = lambda b, i: (0, 0)
    tok = lambda w: pl.BlockSpec((1, tm, w), lambda b, i: (b, i, 0))
    full = lambda a: pl.BlockSpec(a.shape, const)
    mem = pl.BlockSpec((1, n_mem, d), lambda b, i: (b, 0, 0))
    return pl.pallas_call(
        _outattn_body,
        grid=(bsz, t // tm),
        in_specs=[tok(d), tok(A_W), tok(B_W), tok(C_W), full(woa), full(wob), full(woc), full(g),
                  full(wq), full(wo), mem, mem],
        out_specs=tok(d),
        out_shape=jax.ShapeDtypeStruct((bsz, t, d), F32),
        compiler_params=_cparams("parallel", "parallel"),
        name="outattn",
    )(x, oa, ob, yc, woa, wob, woc, g, wq, wo, mk, mv)


def _blockdiag_t(s):
    bsz, h, dk, dv = s.shape
    eye = jnp.eye(h, dtype=s.dtype)
    return jnp.einsum('bhkv,hg->bhvgk', s, eye).reshape(bsz, h * dv, h * dk)


def _unblockdiag_t(m, h, dk, dv):
    bsz = m.shape[0]
    m5 = m.reshape(bsz, h, dv, h, dk)
    return jnp.stack([m5[:, i, :, i, :] for i in range(h)], axis=1).transpose(0, 1, 3, 2)


def _rwkv_pack(s):
    bsz = s.shape[0]
    s4 = s.reshape(bsz, C_HEADS // 2, 2, C_DH, C_DH)
    eye = jnp.eye(2, dtype=s.dtype)
    return jnp.einsum('bphvk,hg->bphvgk', s4, eye).reshape(bsz, C_HEADS // 2, LANES, LANES)


def _rwkv_unpack(m):
    bsz = m.shape[0]
    m6 = m.reshape(bsz, C_HEADS // 2, 2, C_DH, 2, C_DH)
    return jnp.stack([m6[:, :, i, :, i, :] for i in range(2)], axis=2).reshape(bsz, C_HEADS, C_DH, C_DH)


def _rope_tables(pos):
    half = B_DK // 2
    inv = ROPE_BASE ** (-jnp.arange(half, dtype=F32) / half)
    ang = pos.astype(F32)[:, None] * inv[None, :]
    cos = jnp.tile(jnp.cos(ang), (1, 2 * 2 * B_HEADS))
    sin = jnp.tile(jnp.sin(ang), (1, 2 * 2 * B_HEADS))
    return cos, sin


def _rope_swap_matrix():
    n = 2 * B_KW
    half = B_DK // 2
    i = jnp.arange(n)
    within = i % B_DK
    src = jnp.where(within < half, i + half, i - half)
    sign = jnp.where(within < half, -1.0, 1.0)
    return (jnp.zeros((n, n), F32).at[src, i].set(sign)).astype(BF16)


def _prep_layer(l, W, lbs, log_gamma):
    row = lambda a: a.reshape(1, -1).astype(F32)
    w_in = W['w_in'][l]
    pad_rows = lambda m, off: jnp.zeros((C_LORA, C_W), F32).at[off:off + m.shape[0]].set(m).astype(BF16)
    lb = lbs[l]
    zero = jnp.zeros((C_W,), F32)
    vecs = jnp.stack([W['rwkv_w0'][l], W['rwkv_a0'][l], W['rwkv_k_k'][l], W['rwkv_k_a'][l],
                      W['rwkv_r_k'][l].reshape(-1), W['rwkv_ln_w'][l], W['rwkv_ln_b'][l], zero])
    lg_rows = jnp.stack([jnp.pad(jnp.repeat(log_gamma, B_DK), (0, B_W - B_KW)),
                         jnp.repeat(log_gamma, B_DV)])
    w_out = W['w_out'][l].astype(BF16)
    return dict(
        n1=row(W['norm_ffn1'][l]), g1=W['ffn1_wg'][l].astype(BF16), u1=W['ffn1_wu'][l].astype(BF16),
        d1=W['ffn1_wd'][l].astype(BF16),
        n2=row(W['norm_ffn2'][l]), g2=W['ffn2_wg'][l].astype(BF16), u2=W['ffn2_wu'][l].astype(BF16),
        d2=W['ffn2_wd'][l].astype(BF16),
        nmix=row(W['norm_mix'][l]),
        wa=w_in[:, :A_COLS].astype(BF16), wb=w_in[:, A_COLS:A_COLS + B_COLS].astype(BF16),
        wc=w_in[:, A_COLS + B_COLS:].astype(BF16),
        log_lb=row(jnp.log(jnp.maximum(lb, LB_FLOOR))), log1m_lb=row(jnp.log1p(-lb)),
        gn=row(jnp.tile(W['hgrn_norm'][l], A_HEADS)),
        lg_rows=lg_rows, log_gamma=log_gamma,
        mu=row(W['rwkv_mu'][l]), vecs=vecs,
        w2p=pad_rows(W['rwkv_w2'][l], 0), a2p=pad_rows(W['rwkv_a2'][l], C_DECAY_RANK),
        g2p=pad_rows(W['rwkv_g2'][l], C_DECAY_RANK + C_AAA_RANK),
        woa=w_out[:A_W], wob=w_out[A_W:A_W + B_W], woc=w_out[A_W + B_W:],
        ncross=row(W['norm_cross'][l]), wq=W['wq_x'][l].astype(BF16), wo=W['wo_x'][l].astype(BF16),
    )


def _trunk(x, pos, mem_k, mem_v, st_a, st_b, st_c, st_sh, layers, final_g, swap):
    bsz, t, d = x.shape
    n = bsz * t
    depth = len(layers)
    cos_t, sin_t = _rope_tables(pos)
    new_a, new_b, new_c, new_sh = [], [], [], []
    x2 = x.reshape(n, d)
    for l, p in enumerate(layers):
        x2 = _ffn(x2, p['n1'], p['g1'], p['u1'], p['d1'], final_g, False)
        pa, pb, pc = _inproj(x2, p['nmix'], p['wa'], p['wb'], p['wc'])
        oa, ma = _hgrn(pa.reshape(bsz, t, A_COLS), p['log_lb'], p['log1m_lb'], p['gn'],
                       _blockdiag_t(st_a[l]))
        ob, mb = _retention(p['log_gamma'], pb.reshape(bsz, t, B_COLS), cos_t, sin_t, swap, p['lg_rows'],
                            _blockdiag_t(st_b[l]))
        yc, sc, sh = _rwkv(pc.reshape(bsz, t, C_COLS), st_sh[l], _rwkv_pack(st_c[l]), p['mu'],
                           p['vecs'], p['w2p'], p['a2p'], p['g2p'])
        x3 = _outattn(x2.reshape(bsz, t, d), oa, ob, yc, p['woa'], p['wob'], p['woc'], p['ncross'],
                      p['wq'], p['wo'], mem_k[l].reshape(bsz, -1, d), mem_v[l].reshape(bsz, -1, d))
        x2 = _ffn(x3.reshape(n, d), p['n2'], p['g2'], p['u2'], p['d2'], final_g, l == depth - 1)
        new_a.append(_unblockdiag_t(ma, A_HEADS, A_DK, A_DV))
        new_b.append(_unblockdiag_t(mb, B_HEADS, B_DK, B_DV))
        new_c.append(_rwkv_unpack(sc))
        new_sh.append(sh)
    return (x2.reshape(bsz, t, d), jnp.stack(new_a), jnp.stack(new_b), jnp.stack(new_c),
            jnp.stack(new_sh))


def kernel(x_prompt, x_sample, mem_prompt, cache_mem_k, cache_mem_v, state_hgrn, state_ret, state_rwkv, state_rwkv_shift, norm_ffn1, ffn1_wg, ffn1_wu, ffn1_wd, norm_mix, w_in, hgrn_lb_param, hgrn_norm, rwkv_mu, rwkv_w0, rwkv_w2, rwkv_a0, rwkv_a2, rwkv_g2, rwkv_k_k, rwkv_k_a, rwkv_r_k, rwkv_ln_w, rwkv_ln_b, w_out, norm_cross, norm_mem, wq_x, wk_x, wv_x, wo_x, norm_ffn2, ffn2_wg, ffn2_wu, ffn2_wd, final_norm):
    W = dict(norm_ffn1=norm_ffn1, ffn1_wg=ffn1_wg, ffn1_wu=ffn1_wu, ffn1_wd=ffn1_wd, norm_mix=norm_mix,
             w_in=w_in, hgrn_norm=hgrn_norm, rwkv_mu=rwkv_mu, rwkv_w0=rwkv_w0, rwkv_w2=rwkv_w2,
             rwkv_a0=rwkv_a0, rwkv_a2=rwkv_a2, rwkv_g2=rwkv_g2, rwkv_k_k=rwkv_k_k, rwkv_k_a=rwkv_k_a,
             rwkv_r_k=rwkv_r_k, rwkv_ln_w=rwkv_ln_w, rwkv_ln_b=rwkv_ln_b, w_out=w_out,
             norm_cross=norm_cross, wq_x=wq_x, wo_x=wo_x, norm_ffn2=norm_ffn2, ffn2_wg=ffn2_wg,
             ffn2_wu=ffn2_wu, ffn2_wd=ffn2_wd)
    depth = w_in.shape[0]
    d = x_prompt.shape[-1]
    sm = jax.nn.softmax(hgrn_lb_param.astype(F32), axis=0)
    lbs = jnp.cumsum(sm, axis=0) - sm[0:1]
    log_gamma = jnp.log1p(-jnp.exp2(-5.0 - jnp.arange(B_HEADS, dtype=F32)))
    layers = [_prep_layer(l, W, lbs, log_gamma) for l in range(depth)]
    final_g = final_norm.reshape(1, -1).astype(F32)
    swap = _rope_swap_matrix()

    bp, tp, _ = x_prompt.shape
    n_mem = mem_prompt.shape[1]
    hd = d // MEM_HEADS
    mem2 = mem_prompt.reshape(bp * n_mem, d)
    mks, mvs = [], []
    for l in range(depth):
        mk, mv = _memkv(mem2, norm_mem[l].reshape(1, -1), wk_x[l].astype(BF16), wv_x[l].astype(BF16))
        mks.append(mk.reshape(bp, n_mem, MEM_HEADS, hd))
        mvs.append(mv.reshape(bp, n_mem, MEM_HEADS, hd))
    p_mem_k = jnp.stack(mks)
    p_mem_v = jnp.stack(mvs)

    zeros = lambda *s: jnp.zeros((depth, bp) + s, F32)
    y_prompt, p_hgrn, p_ret, p_rwkv, p_shift = _trunk(
        x_prompt, jnp.arange(tp, dtype=jnp.int32), p_mem_k, p_mem_v,
        zeros(A_HEADS, A_DK, A_DV), zeros(B_HEADS, B_DK, B_DV), zeros(C_HEADS, C_DH, C_DH),
        zeros(1, C_COLS), layers, final_g, swap)

    ts = x_sample.shape[1]
    y_sample, s_hgrn, s_ret, s_rwkv, s_shift = _trunk(
        x_sample, PAST_LEN + jnp.arange(ts, dtype=jnp.int32), cache_mem_k, cache_mem_v,
        state_hgrn, state_ret, state_rwkv, state_rwkv_shift, layers, final_g, swap)

    return (y_prompt, y_sample, p_hgrn, p_ret, p_rwkv, p_shift, p_mem_k, p_mem_v,
            s_hgrn, s_ret, s_rwkv, s_shift)
```

```python
import functools

import jax
import jax.numpy as jnp
from jax import lax
from jax.experimental import pallas as pl
from jax.experimental.pallas import tpu as pltpu

F32 = jnp.float32
BF16 = jnp.bfloat16

EPS = 1e-6
LB_FLOOR = 1e-20
ROPE_BASE = 10000.0
PAST_LEN = 4096
MEM_HEADS = 4
A_HEADS, A_DK, A_DV = 4, 64, 64
A_KW = A_HEADS * A_DK
A_W = A_HEADS * A_DV
A_COLS = 2 * A_KW + 2 * A_W
B_HEADS, B_DK, B_DV = 6, 32, 64
B_KW = B_HEADS * B_DK
B_W = B_HEADS * B_DV
B_COLS = 2 * B_KW + 2 * B_W
C_HEADS, C_DH = 6, 64
C_W = C_HEADS * C_DH
C_DECAY_RANK, C_AAA_RANK, C_GATE_RANK = 32, 32, 64
C_LORA = C_DECAY_RANK + C_AAA_RANK + C_GATE_RANK
C_COLS = 3 * C_W + C_LORA
RWKV_GN_EPS = 64e-5

LANES = 128
VMEM_LIMIT = 56 * 1024 * 1024
FFN_TILE = 512
FFN_FCHUNK = 256
TOK_TILE = 512
HGRN_BLOCK = 16
HGRN_TILE = 128
RET_TILE = 256
RWKV_CHUNK = 64
RWKV_TILE = 128


def _cparams(*sem):
    return pltpu.CompilerParams(dimension_semantics=sem, vmem_limit_bytes=VMEM_LIMIT)


def _mm(a, b):
    return jnp.dot(a.astype(BF16), b.astype(BF16), preferred_element_type=F32)


def _mm_nt(a, b):
    return lax.dot_general(a.astype(BF16), b.astype(BF16), (((1,), (1,)), ((), ())),
                           preferred_element_type=F32)


def _mm_tn(a, b):
    return lax.dot_general(a.astype(BF16), b.astype(BF16), (((0,), (0,)), ((), ())),
                           preferred_element_type=F32)


def _split3(x):
    hi = x.astype(BF16)
    r = x - hi.astype(F32)
    mid = r.astype(BF16)
    lo = (r - mid.astype(F32)).astype(BF16)
    return hi, mid, lo


def _mm_exact_lhs(c, x):
    hi, mid, lo = _split3(x)
    return (jnp.dot(c, hi, preferred_element_type=F32) + jnp.dot(c, mid, preferred_element_type=F32)
            + jnp.dot(c, lo, preferred_element_type=F32))


def _mm_exact_rhs(x, c):
    hi, mid, lo = _split3(x)
    return (jnp.dot(hi, c, preferred_element_type=F32) + jnp.dot(mid, c, preferred_element_type=F32)
            + jnp.dot(lo, c, preferred_element_type=F32))


def _rms(x, g):
    return x * lax.rsqrt(jnp.mean(x * x, axis=-1, keepdims=True) + EPS) * g


def _sigmoid(x):
    return 1.0 / (1.0 + jnp.exp(-x))


def _silu(x):
    return x * _sigmoid(x)


def _softplus(x):
    return jnp.maximum(x, 0.0) + jnp.log1p(jnp.exp(-jnp.abs(x)))


def _logaddexp(a, b):
    return jnp.maximum(a, b) + jnp.log1p(jnp.exp(-jnp.abs(a - b)))


def _iota2(shape, axis):
    return lax.broadcasted_iota(jnp.int32, shape, axis)


def _tri_incl(n):
    return (_iota2((n, n), 0) >= _iota2((n, n), 1)).astype(BF16)


def _block_ones(n, blk):
    return (_iota2((n, n), 0) // blk) == (_iota2((n, n), 1) // blk)


def _seg_sum(x, blk):
    n = x.shape[-1]
    outs = []
    bo = _block_ones(LANES, blk).astype(BF16)
    for p in range(n // LANES):
        outs.append(_mm_exact_rhs(x[:, p * LANES:(p + 1) * LANES], bo))
    return outs[0] if len(outs) == 1 else jnp.concatenate(outs, axis=-1)


def _ffn_body(x_ref, g_ref, wg_ref, wu_ref, wd_ref, gf_ref, o_ref, *, apply_final_norm):
    x = x_ref[...]
    hb = _rms(x, g_ref[...]).astype(BF16)
    d_ff = wg_ref.shape[1]
    acc = jnp.zeros(x.shape, F32)
    for j in range(d_ff // FFN_FCHUNK):
        cols = slice(j * FFN_FCHUNK, (j + 1) * FFN_FCHUNK)
        a = jnp.dot(hb, wg_ref[:, cols], preferred_element_type=F32)
        u = jnp.dot(hb, wu_ref[:, cols], preferred_element_type=F32)
        t = (_silu(a) * u).astype(BF16)
        acc = acc + jnp.dot(t, wd_ref[cols, :], preferred_element_type=F32)
    y = x + 0.5 * acc
    if apply_final_norm:
        y = _rms(y, gf_ref[...])
    o_ref[...] = y


def _ffn(x, g, wg, wu, wd, gf, apply_final_norm):
    n, d = x.shape
    d_ff = wg.shape[1]
    tm = min(FFN_TILE, n)
    const = lambda i: (0, 0)
    return pl.pallas_call(
        functools.partial(_ffn_body, apply_final_norm=apply_final_norm),
        grid=(n // tm,),
        in_specs=[pl.BlockSpec((tm, d), lambda i: (i, 0)),
                  pl.BlockSpec((1, d), const),
                  pl.BlockSpec((d, d_ff), const),
                  pl.BlockSpec((d, d_ff), const),
                  pl.BlockSpec((d_ff, d), const),
                  pl.BlockSpec((1, d), const)],
        out_specs=pl.BlockSpec((tm, d), lambda i: (i, 0)),
        out_shape=jax.ShapeDtypeStruct((n, d), F32),
        compiler_params=_cparams("parallel"),
        name="ffn",
    )(x, g, wg, wu, wd, gf)


def _inproj_body(x_ref, g_ref, wa_ref, wb_ref, wc_ref, pa_ref, pb_ref, pc_ref):
    hb = _rms(x_ref[...], g_ref[...]).astype(BF16)
    pa_ref[...] = jnp.dot(hb, wa_ref[...], preferred_element_type=F32)
    pb_ref[...] = jnp.dot(hb, wb_ref[...], preferred_element_type=F32)
    pc_ref[...] = jnp.dot(hb, wc_ref[...], preferred_element_type=F32)


def _inproj(x, g, wa, wb, wc):
    n, d = x.shape
    tm = min(TOK_TILE, n)
    const = lambda i: (0, 0)
    row = lambda i: (i, 0)
    widths = (wa.shape[1], wb.shape[1], wc.shape[1])
    return pl.pallas_call(
        _inproj_body,
        grid=(n // tm,),
        in_specs=[pl.BlockSpec((tm, d), row), pl.BlockSpec((1, d), const)]
                 + [pl.BlockSpec((d, w), const) for w in widths],
        out_specs=[pl.BlockSpec((tm, w), row) for w in widths],
        out_shape=[jax.ShapeDtypeStruct((n, w), F32) for w in widths],
        compiler_params=_cparams("parallel"),
        name="inproj",
    )(x, g, wa, wb, wc)


def _memkv_body(m_ref, g_ref, wk_ref, wv_ref, k_ref, v_ref):
    hb = _rms(m_ref[...], g_ref[...]).astype(BF16)
    k_ref[...] = jnp.dot(hb, wk_ref[...], preferred_element_type=F32)
    v_ref[...] = jnp.dot(hb, wv_ref[...], preferred_element_type=F32)


def _memkv(mem, g, wk, wv):
    n, d = mem.shape
    tm = min(TOK_TILE, n)
    const = lambda i: (0, 0)
    row = lambda i: (i, 0)
    return pl.pallas_call(
        _memkv_body,
        grid=(n // tm,),
        in_specs=[pl.BlockSpec((tm, d), row), pl.BlockSpec((1, d), const),
                  pl.BlockSpec((d, d), const), pl.BlockSpec((d, d), const)],
        out_specs=[pl.BlockSpec((tm, d), row), pl.BlockSpec((tm, d), row)],
        out_shape=[jax.ShapeDtypeStruct((n, d), F32)] * 2,
        compiler_params=_cparams("parallel"),
        name="memkv",
    )(mem, g, wk, wv)


def _hgrn_body(pa_ref, loglb_ref, log1mlb_ref, gn_ref, m0_ref, oa_ref, mout_ref, m_scr, *, tile):
    c = pl.program_id(1)
    blk = HGRN_BLOCK
    nb = tile // blk

    @pl.when(c == 0)
    def _():
        m_scr[...] = m0_ref[0]

    x = pa_ref[0]
    q = x[:, 0:A_KW]
    f = x[:, A_KW:2 * A_KW]
    v = x[:, 2 * A_KW:2 * A_KW + A_W]
    g = x[:, 2 * A_KW + A_W:A_COLS]
    logf = _logaddexp(loglb_ref[...], log1mlb_ref[...] - _softplus(-f))
    ka = 1.0 - jnp.exp(logf)
    qa = _silu(q) * (A_DK ** -0.5)
    ti = _iota2((tile, tile), 0)
    si = _iota2((tile, tile), 1)
    same_blk = (ti // blk) == (si // blk)
    b = _mm_exact_lhs((same_blk & (ti >= si)).astype(BF16), logf)
    b_tot = _mm_exact_lhs(same_blk.astype(BF16), logf)
    q_t = qa * jnp.exp(b)
    k_hat = ka * jnp.exp(b_tot - b)
    d_blk = jnp.exp(b_tot)
    head_ones = _block_ones(A_KW, A_DK)
    head_ones_b = head_ones.astype(BF16)

    split = lambda a: a.reshape(nb, blk, A_KW)
    b3, ka3, qa3, v3 = split(b), split(ka), split(qa), split(v)
    row3 = lax.broadcasted_iota(jnp.int32, (nb, blk, A_KW), 1)
    slabs = []
    for s in range(blk):
        z = qa3 * ka3[:, s:s + 1, :] * jnp.exp(jnp.minimum(b3 - b3[:, s:s + 1, :], 0.0))
        slabs.append(jnp.where(row3 >= s, z, 0.0).reshape(tile, A_KW))
    att = _mm(jnp.concatenate(slabs, axis=0), head_ones_b)
    o3 = jnp.zeros((nb, blk, A_W), F32)
    for s in range(blk):
        o3 = o3 + att[s * tile:(s + 1) * tile, :].reshape(nb, blk, A_W) * v3[:, s:s + 1, :]

    rows = lambda a, j: a[j * blk:(j + 1) * blk, :]
    outer = [jnp.where(head_ones, _mm_tn(rows(v, j), rows(k_hat, j)), 0.0) for j in range(nb)]
    m = m_scr[...]
    states = []
    for j in range(nb):
        states.append(m)
        m = m * d_blk[j * blk:j * blk + 1, :] + outer[j]
    m_scr[...] = m
    o = jnp.concatenate([_mm_nt(rows(q_t, j), states[j]) for j in range(nb)], axis=0)
    o = o + o3.reshape(tile, A_W)
    ms = _mm_exact_rhs(o * o, head_ones_b) * (1.0 / A_DV)
    oa_ref[0] = o * lax.rsqrt(ms + EPS) * gn_ref[...] * _silu(g)

    @pl.when(c == pl.num_programs(1) - 1)
    def _():
        mout_ref[0] = m_scr[...]


def _hgrn(pa, log_lb, log1m_lb, gn, m0):
    bsz, t, _ = pa.shape
    tile = min(HGRN_TILE, t)
    const = lambda b, c: (0, 0)
    return pl.pallas_call(
        functools.partial(_hgrn_body, tile=tile),
        grid=(bsz, t // tile),
        in_specs=[pl.BlockSpec((1, tile, A_COLS), lambda b, c: (b, c, 0)),
                  pl.BlockSpec((1, A_KW), const), pl.BlockSpec((1, A_KW), const),
                  pl.BlockSpec((1, A_W), const),
                  pl.BlockSpec((1, A_W, A_KW), lambda b, c: (b, 0, 0))],
        out_specs=[pl.BlockSpec((1, tile, A_W), lambda b, c: (b, c, 0)),
                   pl.BlockSpec((1, A_W, A_KW), lambda b, c: (b, 0, 0))],
        out_shape=[jax.ShapeDtypeStruct((bsz, t, A_W), F32),
                   jax.ShapeDtypeStruct((bsz, A_W, A_KW), F32)],
        scratch_shapes=[pltpu.VMEM((A_W, A_KW), F32)],
        compiler_params=_cparams("parallel", "arbitrary"),
        name="hgrn",
    )(pa, log_lb, log1m_lb, gn, m0)


def _ret_body(lgs_ref, pb_ref, cos_ref, sin_ref, swap_ref, lg_ref, m0_ref, ob_ref, mout_ref, m_scr, *,
              tile):
    c = pl.program_id(1)

    @pl.when(c == 0)
    def _():
        m_scr[...] = m0_ref[0]

    x = pb_ref[0]
    qk = x[:, 0:2 * B_KW]
    v = x[:, 2 * B_KW:2 * B_KW + B_W]
    g = x[:, 2 * B_KW + B_W:B_COLS]
    qk = qk * cos_ref[...] + _mm_exact_rhs(qk, swap_ref[...]) * sin_ref[...]
    q = qk[:, 0:B_KW]
    k = qk[:, B_KW:2 * B_KW] * (B_DK ** -0.5)

    lg_k = lg_ref[0:1, 0:B_KW]
    lg_v = lg_ref[1:2, :]
    j_k = _iota2((tile, B_KW), 0).astype(F32)
    j_v = _iota2((tile, B_W), 0).astype(F32)
    m = m_scr[...]
    cross = _mm_nt(q * jnp.exp(lg_k * (j_k + 1.0)), m)
    rel = (_iota2((tile, tile), 0) - _iota2((tile, tile), 1)).astype(F32)
    outs = []
    for h in range(B_HEADS):
        lg_h = lgs_ref[h]
        dmat = jnp.where(rel >= 0.0, jnp.exp(lg_h * jnp.maximum(rel, 0.0)), 0.0)
        s = _mm_nt(q[:, h * B_DK:(h + 1) * B_DK], k[:, h * B_DK:(h + 1) * B_DK]) * dmat
        outs.append(_mm(s, v[:, h * B_DV:(h + 1) * B_DV]))
    o = jnp.concatenate(outs, axis=-1) + cross
    mu = _seg_sum(o, B_DV) * (1.0 / B_DV)
    oc = o - mu
    var = _seg_sum(oc * oc, B_DV) * (1.0 / B_DV)
    ob_ref[0] = oc * lax.rsqrt(var + EPS) * _silu(g)
    v_w = v * jnp.exp(lg_v * (tile - 1.0 - j_v))
    head_mask = (_iota2((B_W, B_KW), 0) // B_DV) == (_iota2((B_W, B_KW), 1) // B_DK)
    m_scr[...] = m * jnp.exp(lg_k * float(tile)) + jnp.where(head_mask, _mm_tn(v_w, k), 0.0)

    @pl.when(c == pl.num_programs(1) - 1)
    def _():
        mout_ref[0] = m_scr[...]


def _retention(log_gamma, pb, cos_t, sin_t, swap, lg_rows, m0):
    bsz, t, _ = pb.shape
    tile = min(RET_TILE, t)
    const = lambda b, c: (0, 0)
    return pl.pallas_call(
        functools.partial(_ret_body, tile=tile),
        grid=(bsz, t // tile),
        in_specs=[pl.BlockSpec(memory_space=pltpu.SMEM),
                  pl.BlockSpec((1, tile, B_COLS), lambda b, c: (b, c, 0)),
                  pl.BlockSpec((tile, 2 * B_KW), lambda b, c: (c, 0)),
                  pl.BlockSpec((tile, 2 * B_KW), lambda b, c: (c, 0)),
                  pl.BlockSpec((2 * B_KW, 2 * B_KW), const),
                  pl.BlockSpec((2, B_W), const),
                  pl.BlockSpec((1, B_W, B_KW), lambda b, c: (b, 0, 0))],
        out_specs=[pl.BlockSpec((1, tile, B_W), lambda b, c: (b, c, 0)),
                   pl.BlockSpec((1, B_W, B_KW), lambda b, c: (b, 0, 0))],
        out_shape=[jax.ShapeDtypeStruct((bsz, t, B_W), F32),
                   jax.ShapeDtypeStruct((bsz, B_W, B_KW), F32)],
        scratch_shapes=[pltpu.VMEM((B_W, B_KW), F32)],
        compiler_params=_cparams("parallel", "arbitrary"),
        name="retention",
    )(log_gamma, pb, cos_t, sin_t, swap, lg_rows, m0)


def _rwkv_body(pc_ref, sh0_ref, s0_ref, mu_ref, vec_ref, w2_ref, a2_ref, g2_ref,
               yc_ref, sout_ref, shout_ref, s_scr, sh_scr, *, nchunk):
    c = pl.program_id(1)
    L = RWKV_CHUNK
    tl = nchunk * L
    npair = C_HEADS // 2

    @pl.when(c == 0)
    def _():
        s_scr[...] = s0_ref[0]
        sh_scr[...] = sh0_ref[0]

    p = pc_ref[0]
    row = _iota2((tl, C_COLS), 0)
    prev = jnp.where(row == 0, sh_scr[...], pltpu.roll(p, 1, 0))
    xs = p + mu_ref[...] * (prev - p)
    sh_scr[...] = p[tl - 1:tl, :]

    r = xs[:, 0:C_W]
    k = xs[:, C_W:2 * C_W]
    v = xs[:, 2 * C_W:3 * C_W]
    lora = xs[:, 3 * C_W:C_COLS]
    w0, a0, k_k, k_a, r_k, ln_w, ln_b = (vec_ref[i:i + 1, :] for i in range(7))
    w = -_softplus(-(w0 + _mm(jnp.tanh(lora), w2_ref[...]))) - 0.5
    lw = -jnp.exp(w)
    a = _sigmoid(a0 + _mm(lora, a2_ref[...]))
    g = _mm(_sigmoid(lora), g2_ref[...])
    kk = k * k_k
    kk = kk / jnp.maximum(jnp.sqrt(_seg_sum(kk * kk, C_DH)), 1e-6)
    kc = k * (1.0 + (a - 1.0) * k_a)
    beta = kk * a

    ti = _iota2((tl, tl), 0)
    si = _iota2((tl, tl), 1)
    same_chunk = (ti // L) == (si // L)
    b = _mm_exact_lhs((same_chunk & (ti >= si)).astype(BF16), lw)
    b_tot = _mm_exact_lhs(same_chunk.astype(BF16), lw)
    e_nb = jnp.exp(-b)
    e_tail = jnp.exp(b_tot - b)
    kap_t = kk * jnp.exp(b - lw)
    r_t = r * jnp.exp(b)
    beta_t = beta * e_nb
    k_t = kc * e_nb
    beta_h = beta * e_tail
    k_h = kc * e_tail
    p_last = jnp.exp(b_tot)

    tt = _iota2((L, L), 0)
    ss = _iota2((L, L), 1)
    strict = tt > ss
    incl2 = _iota2((L, 2 * L), 0) >= (_iota2((L, 2 * L), 1) % L)
    eye = (tt == ss).astype(F32)
    head0 = (_iota2((L, LANES), 1) // C_DH) == 0
    head0_2 = ((_iota2((L, 2 * LANES), 1) % LANES) // C_DH) == 0
    hmask = [head0, jnp.logical_not(head0)]
    pair_diag = _block_ones(LANES, C_DH)
    zeros_p = jnp.zeros((L, LANES), F32)

    cut = lambda x, ci, pr: x[ci * L:(ci + 1) * L, pr * LANES:(pr + 1) * LANES]
    items = [(ci, pr) for ci in range(nchunk) for pr in range(npair)]
    keys = [(ci, pr, hh) for (ci, pr) in items for hh in range(2)]

    gram = {}
    for ci, pr in items:
        bk = jnp.concatenate([cut(beta_t, ci, pr), cut(k_t, ci, pr)], axis=0)
        for hh in range(2):
            lhs = jnp.concatenate([jnp.where(hmask[hh], cut(kap_t, ci, pr), 0.0),
                                   jnp.where(hmask[hh], cut(r_t, ci, pr), 0.0)], axis=0)
            gram[ci, pr, hh] = _mm_nt(lhs, bk)
    n_pow = {key: -jnp.where(strict, gram[key][0:L, 0:L], 0.0) for key in keys}
    b_m = {key: jnp.where(strict, gram[key][0:L, L:2 * L], 0.0) for key in keys}
    ce_m = {key: jnp.where(incl2, gram[key][L:2 * L, :], 0.0) for key in keys}

    t_inv = {key: eye + n_pow[key] for key in keys}
    for _ in range(L.bit_length() - 2):
        n_pow = {key: _mm(n_pow[key], n_pow[key]) for key in keys}
        upd = {key: _mm(n_pow[key], t_inv[key]) for key in keys}
        t_inv = {key: t_inv[key] + upd[key] for key in keys}

    bv = {key: _mm(b_m[key], cut(v, key[0], key[1])) for key in keys}
    res = {}
    for ci, pr in items:
        bv_p = jnp.where(head0, bv[ci, pr, 0], bv[ci, pr, 1])
        rhs = jnp.concatenate([bv_p, cut(kap_t, ci, pr)], axis=1)
        for hh in range(2):
            res[ci, pr, hh] = _mm(t_inv[ci, pr, hh], rhs)
    u0, wm = {}, {}
    for it in items:
        both = jnp.where(head0_2, res[it + (0,)], res[it + (1,)])
        u0[it] = -both[:, 0:LANES]
        wm[it] = both[:, LANES:2 * LANES]
    for ci, pr in items:
        rhs = jnp.concatenate([jnp.concatenate([u0[ci, pr], wm[ci, pr]], axis=1),
                               jnp.concatenate([cut(v, ci, pr), zeros_p], axis=1)], axis=0)
        for hh in range(2):
            res[ci, pr, hh] = _mm(ce_m[ci, pr, hh], rhs)
    y0, r_hat = {}, {}
    for it in items:
        both = jnp.where(head0_2, res[it + (0,)], res[it + (1,)])
        y0[it] = both[:, 0:LANES]
        r_hat[it] = cut(r_t, *it) - both[:, LANES:2 * LANES]
    q_m = {it: -jnp.where(pair_diag, _mm_tn(wm[it], cut(beta_h, *it)), 0.0) for it in items}
    z_m = {it: jnp.where(pair_diag,
                         _mm_tn(jnp.concatenate([u0[it], cut(v, *it)], axis=0),
                                jnp.concatenate([cut(beta_h, *it), cut(k_h, *it)], axis=0)), 0.0)
           for it in items}

    s_mat = [s_scr[pr] for pr in range(npair)]
    y_rows = []
    for ci in range(nchunk):
        y_dep = [_mm_nt(r_hat[ci, pr], s_mat[pr]) for pr in range(npair)]
        s_q = [_mm(s_mat[pr], q_m[ci, pr]) for pr in range(npair)]
        s_mat = [s_mat[pr] * p_last[ci * L:ci * L + 1, pr * LANES:(pr + 1) * LANES] + s_q[pr] + z_m[ci, pr]
                 for pr in range(npair)]
        y_rows.append(jnp.concatenate([y_dep[pr] + y0[ci, pr] for pr in range(npair)], axis=-1))
    for pr in range(npair):
        s_scr[pr] = s_mat[pr]

    y = y_rows[0] if nchunk == 1 else jnp.concatenate(y_rows, axis=0)
    mu_y = _seg_sum(y, C_DH) * (1.0 / C_DH)
    yc = y - mu_y
    var = _seg_sum(yc * yc, C_DH) * (1.0 / C_DH)
    yn = yc * lax.rsqrt(var + RWKV_GN_EPS) * ln_w + ln_b
    bonus = _seg_sum(r * kc * r_k, C_DH) * v
    yc_ref[0] = (yn + bonus) * g

    @pl.when(c == pl.num_programs(1) - 1)
    def _():
        sout_ref[0] = s_scr[...]
        shout_ref[0] = sh_scr[...]


def _rwkv(pc, sh0, s0, mu, vecs, w2p, a2p, g2p):
    bsz, t, _ = pc.shape
    L = min(RWKV_TILE, t)
    npair = C_HEADS // 2
    const = lambda b, c: (0, 0)
    return pl.pallas_call(
        functools.partial(_rwkv_body, nchunk=L // RWKV_CHUNK),
        grid=(bsz, t // L),
        in_specs=[pl.BlockSpec((1, L, C_COLS), lambda b, c: (b, c, 0)),
                  pl.BlockSpec((1, 1, C_COLS), lambda b, c: (b, 0, 0)),
                  pl.BlockSpec((1, npair, LANES, LANES), lambda b, c: (b, 0, 0, 0)),
                  pl.BlockSpec((1, C_COLS), const),
                  pl.BlockSpec((8, C_W), const),
                  pl.BlockSpec((C_LORA, C_W), const),
                  pl.BlockSpec((C_LORA, C_W), const),
                  pl.BlockSpec((C_LORA, C_W), const)],
        out_specs=[pl.BlockSpec((1, L, C_W), lambda b, c: (b, c, 0)),
                   pl.BlockSpec((1, npair, LANES, LANES), lambda b, c: (b, 0, 0, 0)),
                   pl.BlockSpec((1, 1, C_COLS), lambda b, c: (b, 0, 0))],
        out_shape=[jax.ShapeDtypeStruct((bsz, t, C_W), F32),
                   jax.ShapeDtypeStruct((bsz, npair, LANES, LANES), F32),
                   jax.ShapeDtypeStruct((bsz, 1, C_COLS), F32)],
        scratch_shapes=[pltpu.VMEM((npair, LANES, LANES), F32), pltpu.VMEM((1, C_COLS), F32)],
        compiler_params=_cparams("parallel", "arbitrary"),
        name="rwkv",
    )(pc, sh0, s0, mu, vecs, w2p, a2p, g2p)


def _outattn_body(x_ref, oa_ref, ob_ref, yc_ref, woa_ref, wob_ref, woc_ref, g_ref, wq_ref, wo_ref,
                  mk_ref, mv_ref, o_ref):
    x = (x_ref[0] + _mm(oa_ref[0], woa_ref[...]) + _mm(ob_ref[0], wob_ref[...])
         + _mm(yc_ref[0], woc_ref[...]))
    q = _mm(_rms(x, g_ref[...]), wq_ref[...])
    d = x.shape[-1]
    hd = d // MEM_HEADS
    outs = []
    for h in range(MEM_HEADS):
        cols = slice(h * hd, (h + 1) * hd)
        s = _mm_nt(q[:, cols], mk_ref[0, :, cols]) * (hd ** -0.5)
        s = s - jnp.max(s, axis=-1, keepdims=True)
        e = jnp.exp(s)
        pr = e / jnp.sum(e, axis=-1, keepdims=True)
        outs.append(_mm(pr, mv_ref[0, :, cols]))
    o_ref[0] = x + _mm(jnp.concatenate(outs, axis=-1), wo_ref[...])


def _outattn(x, oa, ob, yc, woa, wob, woc, g, wq, wo, mk, mv):
    bsz, t, d = x.shape
    tm = min(TOK_TILE, t)
    n_mem = mk.shape[1]
    const = lambda b, i: (0, 0)
    tok = lambda w: pl.BlockSpec((1, tm, w), lambda b, i: (b, i, 0))
    full = lambda a: pl.BlockSpec(a.shape, const)
    mem = pl.BlockSpec((1, n_mem, d), lambda b, i: (b, 0, 0))
    return pl.pallas_call(
        _outattn_body,
        grid=(bsz, t // tm),
        in_specs=[tok(d), tok(A_W), tok(B_W), tok(C_W), full(woa), full(wob), full(woc), full(g),
                  full(wq), full(wo), mem, mem],
        out_specs=tok(d),
        out_shape=jax.ShapeDtypeStruct((bsz, t, d), F32),
        compiler_params=_cparams("parallel", "parallel"),
        name="outattn",
    )(x, oa, ob, yc, woa, wob, woc, g, wq, wo, mk, mv)


def _blockdiag_t(s):
    bsz, h, dk, dv = s.shape
    eye = jnp.eye(h, dtype=s.dtype)
    return jnp.einsum('bhkv,hg->bhvgk', s, eye).reshape(bsz, h * dv, h * dk)


def _unblockdiag_t(m, h, dk, dv):
    bsz = m.shape[0]
    m5 = m.reshape(bsz, h, dv, h, dk)
    return jnp.stack([m5[:, i, :, i, :] for i in range(h)], axis=1).transpose(0, 1, 3, 2)


def _rwkv_pack(s):
    bsz = s.shape[0]
    s4 = s.reshape(bsz, C_HEADS // 2, 2, C_DH, C_DH)
    eye = jnp.eye(2, dtype=s.dtype)
    return jnp.einsum('bphvk,hg->bphvgk', s4, eye).reshape(bsz, C_HEADS // 2, LANES, LANES)


def _rwkv_unpack(m):
    bsz = m.shape[0]
    m6 = m.reshape(bsz, C_HEADS // 2, 2, C_DH, 2, C_DH)
    return jnp.stack([m6[:, :, i, :, i, :] for i in range(2)], axis=2).reshape(bsz, C_HEADS, C_DH, C_DH)


def _rope_tables(pos):
    half = B_DK // 2
    inv = ROPE_BASE ** (-jnp.arange(half, dtype=F32) / half)
    ang = pos.astype(F32)[:, None] * inv[None, :]
    cos = jnp.tile(jnp.cos(ang), (1, 2 * 2 * B_HEADS))
    sin = jnp.tile(jnp.sin(ang), (1, 2 * 2 * B_HEADS))
    return cos, sin


def _rope_swap_matrix():
    n = 2 * B_KW
    half = B_DK // 2
    i = jnp.arange(n)
    within = i % B_DK
    src = jnp.where(within < half, i + half, i - half)
    sign = jnp.where(within < half, -1.0, 1.0)
    return (jnp.zeros((n, n), F32).at[src, i].set(sign)).astype(BF16)


def _prep_layer(l, W, lbs, log_gamma):
    row = lambda a: a.reshape(1, -1).astype(F32)
    w_in = W['w_in'][l]
    pad_rows = lambda m, off: jnp.zeros((C_LORA, C_W), F32).at[off:off + m.shape[0]].set(m).astype(BF16)
    lb = lbs[l]
    zero = jnp.zeros((C_W,), F32)
    vecs = jnp.stack([W['rwkv_w0'][l], W['rwkv_a0'][l], W['rwkv_k_k'][l], W['rwkv_k_a'][l],
                      W['rwkv_r_k'][l].reshape(-1), W['rwkv_ln_w'][l], W['rwkv_ln_b'][l], zero])
    lg_rows = jnp.stack([jnp.pad(jnp.repeat(log_gamma, B_DK), (0, B_W - B_KW)),
                         jnp.repeat(log_gamma, B_DV)])
    w_out = W['w_out'][l].astype(BF16)
    return dict(
        n1=row(W['norm_ffn1'][l]), g1=W['ffn1_wg'][l].astype(BF16), u1=W['ffn1_wu'][l].astype(BF16),
        d1=W['ffn1_wd'][l].astype(BF16),
        n2=row(W['norm_ffn2'][l]), g2=W['ffn2_wg'][l].astype(BF16), u2=W['ffn2_wu'][l].astype(BF16),
        d2=W['ffn2_wd'][l].astype(BF16),
        nmix=row(W['norm_mix'][l]),
        wa=w_in[:, :A_COLS].astype(BF16), wb=w_in[:, A_COLS:A_COLS + B_COLS].astype(BF16),
        wc=w_in[:, A_COLS + B_COLS:].astype(BF16),
        log_lb=row(jnp.log(jnp.maximum(lb, LB_FLOOR))), log1m_lb=row(jnp.log1p(-lb)),
        gn=row(jnp.tile(W['hgrn_norm'][l], A_HEADS)),
        lg_rows=lg_rows, log_gamma=log_gamma,
        mu=row(W['rwkv_mu'][l]), vecs=vecs,
        w2p=pad_rows(W['rwkv_w2'][l], 0), a2p=pad_rows(W['rwkv_a2'][l], C_DECAY_RANK),
        g2p=pad_rows(W['rwkv_g2'][l], C_DECAY_RANK + C_AAA_RANK),
        woa=w_out[:A_W], wob=w_out[A_W:A_W + B_W], woc=w_out[A_W + B_W:],
        ncross=row(W['norm_cross'][l]), wq=W['wq_x'][l].astype(BF16), wo=W['wo_x'][l].astype(BF16),
    )


def _trunk(x, pos, mem_k, mem_v, st_a, st_b, st_c, st_sh, layers, final_g, swap):
    bsz, t, d = x.shape
    n = bsz * t
    depth = len(layers)
    cos_t, sin_t = _rope_tables(pos)
    new_a, new_b, new_c, new_sh = [], [], [], []
    x2 = x.reshape(n, d)
    for l, p in enumerate(layers):
        x2 = _ffn(x2, p['n1'], p['g1'], p['u1'], p['d1'], final_g, False)
        pa, pb, pc = _inproj(x2, p['nmix'], p['wa'], p['wb'], p['wc'])
        oa, ma = _hgrn(pa.reshape(bsz, t, A_COLS), p['log_lb'], p['log1m_lb'], p['gn'],
                       _blockdiag_t(st_a[l]))
        ob, mb = _retention(p['log_gamma'], pb.reshape(bsz, t, B_COLS), cos_t, sin_t, swap, p['lg_rows'],
                            _blockdiag_t(st_b[l]))
        yc, sc, sh = _rwkv(pc.reshape(bsz, t, C_COLS), st_sh[l], _rwkv_pack(st_c[l]), p['mu'],
                           p['vecs'], p['w2p'], p['a2p'], p['g2p'])
        x3 = _outattn(x2.reshape(bsz, t, d), oa, ob, yc, p['woa'], p['wob'], p['woc'], p['ncross'],
                      p['wq'], p['wo'], mem_k[l].reshape(bsz, -1, d), mem_v[l].reshape(bsz, -1, d))
        x2 = _ffn(x3.reshape(n, d), p['n2'], p['g2'], p['u2'], p['d2'], final_g, l == depth - 1)
        new_a.append(_unblockdiag_t(ma, A_HEADS, A_DK, A_DV))
        new_b.append(_unblockdiag_t(mb, B_HEADS, B_DK, B_DV))
        new_c.append(_rwkv_unpack(sc))
        new_sh.append(sh)
    return (x2.reshape(bsz, t, d), jnp.stack(new_a), jnp.stack(new_b), jnp.stack(new_c),
            jnp.stack(new_sh))


def kernel(x_prompt, x_sample, mem_prompt, cache_mem_k, cache_mem_v, state_hgrn, state_ret, state_rwkv, state_rwkv_shift, norm_ffn1, ffn1_wg, ffn1_wu, ffn1_wd, norm_mix, w_in, hgrn_lb_param, hgrn_norm, rwkv_mu, rwkv_w0, rwkv_w2, rwkv_a0, rwkv_a2, rwkv_g2, rwkv_k_k, rwkv_k_a, rwkv_r_k, rwkv_ln_w, rwkv_ln_b, w_out, norm_cross, norm_mem, wq_x, wk_x, wv_x, wo_x, norm_ffn2, ffn2_wg, ffn2_wu, ffn2_wd, final_norm):
    W = dict(norm_ffn1=norm_ffn1, ffn1_wg=ffn1_wg, ffn1_wu=ffn1_wu, ffn1_wd=ffn1_wd, norm_mix=norm_mix,
             w_in=w_in, hgrn_norm=hgrn_norm, rwkv_mu=rwkv_mu, rwkv_w0=rwkv_w0, rwkv_w2=rwkv_w2,
             rwkv_a0=rwkv_a0, rwkv_a2=rwkv_a2, rwkv_g2=rwkv_g2, rwkv_k_k=rwkv_k_k, rwkv_k_a=rwkv_k_a,
             rwkv_r_k=rwkv_r_k, rwkv_ln_w=rwkv_ln_w, rwkv_ln_b=rwkv_ln_b, w_out=w_out,
             norm_cross=norm_cross, wq_x=wq_x, wo_x=wo_x, norm_ffn2=norm_ffn2, ffn2_wg=ffn2_wg,
             ffn2_wu=ffn2_wu, ffn2_wd=ffn2_wd)
    depth = w_in.shape[0]
    d = x_prompt.shape[-1]
    sm = jax.nn.softmax(hgrn_lb_param.astype(F32), axis=0)
    lbs = jnp.cumsum(sm, axis=0) - sm[0:1]
    log_gamma = jnp.log1p(-jnp.exp2(-5.0 - jnp.arange(B_HEADS, dtype=F32)))
    layers = [_prep_layer(l, W, lbs, log_gamma) for l in range(depth)]
    final_g = final_norm.reshape(1, -1).astype(F32)
    swap = _rope_swap_matrix()

    bp, tp, _ = x_prompt.shape
    n_mem = mem_prompt.shape[1]
    hd = d // MEM_HEADS
    mem2 = mem_prompt.reshape(bp * n_mem, d)
    mks, mvs = [], []
    for l in range(depth):
        mk, mv = _memkv(mem2, norm_mem[l].reshape(1, -1), wk_x[l].astype(BF16), wv_x[l].astype(BF16))
        mks.append(mk.reshape(bp, n_mem, MEM_HEADS, hd))
        mvs.append(mv.reshape(bp, n_mem, MEM_HEADS, hd))
    p_mem_k = jnp.stack(mks)
    p_mem_v = jnp.stack(mvs)

    zeros = lambda *s: jnp.zeros((depth, bp) + s, F32)
    y_prompt, p_hgrn, p_ret, p_rwkv, p_shift = _trunk(
        x_prompt, jnp.arange(tp, dtype=jnp.int32), p_mem_k, p_mem_v,
        zeros(A_HEADS, A_DK, A_DV), zeros(B_HEADS, B_DK, B_DV), zeros(C_HEADS, C_DH, C_DH),
        zeros(1, C_COLS), layers, final_g, swap)

    ts = x_sample.shape[1]
    y_sample, s_hgrn, s_ret, s_rwkv, s_shift = _trunk(
        x_sample, PAST_LEN + jnp.arange(ts, dtype=jnp.int32), cache_mem_k, cache_mem_v,
        state_hgrn, state_ret, state_rwkv, state_rwkv_shift, layers, final_g, swap)

    return (y_prompt, y_sample, p_hgrn, p_ret, p_rwkv, p_shift, p_mem_k, p_mem_v,
            s_hgrn, s_ret, s_rwkv, s_shift)
```

```python
import functools

import jax
import jax.numpy as jnp
from jax import lax
from jax.experimental import pallas as pl
from jax.experimental.pallas import tpu as pltpu

F32 = jnp.float32
BF16 = jnp.bfloat16

EPS = 1e-6
LB_FLOOR = 1e-20
ROPE_BASE = 10000.0
PAST_LEN = 4096
MEM_HEADS = 4
A_HEADS, A_DK, A_DV = 4, 64, 64
A_KW = A_HEADS * A_DK
A_W = A_HEADS * A_DV
A_COLS = 2 * A_KW + 2 * A_W
B_HEADS, B_DK, B_DV = 6, 32, 64
B_KW = B_HEADS * B_DK
B_W = B_HEADS * B_DV
B_COLS = 2 * B_KW + 2 * B_W
C_HEADS, C_DH = 6, 64
C_W = C_HEADS * C_DH
C_DECAY_RANK, C_AAA_RANK, C_GATE_RANK = 32, 32, 64
C_LORA = C_DECAY_RANK + C_AAA_RANK + C_GATE_RANK
C_COLS = 3 * C_W + C_LORA
RWKV_GN_EPS = 64e-5
LOG2E = 1.4426950408889634

LANES = 128
VMEM_LIMIT = 56 * 1024 * 1024
FFN_TILE = 512
FFN_FCHUNK = 256
TOK_TILE = 512
HGRN_BLOCK = 16
HGRN_TILE = 256
RET_TILE = 256
RWKV_CHUNK = 64
RWKV_TILE = 256
RWKV_GROUP_CHUNKS = 2


def _cparams(*sem):
    return pltpu.CompilerParams(dimension_semantics=sem, vmem_limit_bytes=VMEM_LIMIT)


def _mm(a, b):
    return jnp.dot(a.astype(BF16), b.astype(BF16), preferred_element_type=F32)


def _mm_nt(a, b):
    return lax.dot_general(a.astype(BF16), b.astype(BF16), (((1,), (1,)), ((), ())),
                           preferred_element_type=F32)


def _mm_tn(a, b):
    return lax.dot_general(a.astype(BF16), b.astype(BF16), (((0,), (0,)), ((), ())),
                           preferred_element_type=F32)


def _split3(x):
    hi = x.astype(BF16)
    r = x - hi.astype(F32)
    mid = r.astype(BF16)
    lo = (r - mid.astype(F32)).astype(BF16)
    return hi, mid, lo


def _mm_exact_lhs(c, x):
    hi, mid, lo = _split3(x)
    return (jnp.dot(c, hi, preferred_element_type=F32) + jnp.dot(c, mid, preferred_element_type=F32)
            + jnp.dot(c, lo, preferred_element_type=F32))


def _rms(x, g):
    return x * lax.rsqrt(jnp.mean(x * x, axis=-1, keepdims=True) + EPS) * g


def _sigmoid(x):
    return 1.0 / (1.0 + jnp.exp(-x))


def _silu(x):
    return x * _sigmoid(x)


def _softplus(x):
    return jnp.maximum(x, 0.0) + jnp.log(1.0 + jnp.exp(-jnp.abs(x)))


def _logaddexp(a, b):
    return jnp.maximum(a, b) + jnp.log(1.0 + jnp.exp(-jnp.abs(a - b)))


def _iota2(shape, axis):
    return lax.broadcasted_iota(jnp.int32, shape, axis)


def _block_ones(n, blk):
    return (_iota2((n, n), 0) // blk) == (_iota2((n, n), 1) // blk)


def _seg_sum(x, blk):
    n = x.shape[-1]
    outs = []
    for lo in range(0, n, LANES):
        w = min(LANES, n - lo)
        outs.append(jnp.dot(x[:, lo:lo + w].astype(BF16), _block_ones(w, blk).astype(BF16),
                            preferred_element_type=F32))
    return outs[0] if len(outs) == 1 else jnp.concatenate(outs, axis=-1)


def _ffn_body(x_ref, g_ref, wg_ref, wu_ref, wd_ref, gf_ref, o_ref, *, apply_final_norm):
    x = x_ref[...]
    hb = _rms(x, g_ref[...]).astype(BF16)
    d_ff = wg_ref.shape[1]
    acc = jnp.zeros(x.shape, F32)
    for j in range(d_ff // FFN_FCHUNK):
        cols = slice(j * FFN_FCHUNK, (j + 1) * FFN_FCHUNK)
        a = jnp.dot(hb, wg_ref[:, cols], preferred_element_type=F32)
        u = jnp.dot(hb, wu_ref[:, cols], preferred_element_type=F32)
        t = (_silu(a) * u).astype(BF16)
        acc = acc + jnp.dot(t, wd_ref[cols, :], preferred_element_type=F32)
    y = x + 0.5 * acc
    if apply_final_norm:
        y = _rms(y, gf_ref[...])
    o_ref[...] = y


def _ffn(x, g, wg, wu, wd, gf, apply_final_norm):
    n, d = x.shape
    d_ff = wg.shape[1]
    tm = min(FFN_TILE, n)
    const = lambda i: (0, 0)
    return pl.pallas_call(
        functools.partial(_ffn_body, apply_final_norm=apply_final_norm),
        grid=(n // tm,),
        in_specs=[pl.BlockSpec((tm, d), lambda i: (i, 0)),
                  pl.BlockSpec((1, d), const),
                  pl.BlockSpec((d, d_ff), const),
                  pl.BlockSpec((d, d_ff), const),
                  pl.BlockSpec((d_ff, d), const),
                  pl.BlockSpec((1, d), const)],
        out_specs=pl.BlockSpec((tm, d), lambda i: (i, 0)),
        out_shape=jax.ShapeDtypeStruct((n, d), F32),
        compiler_params=_cparams("parallel"),
        name="ffn",
    )(x, g, wg, wu, wd, gf)


def _inproj_body(x_ref, g_ref, wa_ref, wb_ref, wc_ref, pa_ref, pb_ref, pc_ref):
    hb = _rms(x_ref[...], g_ref[...]).astype(BF16)
    pa_ref[...] = jnp.dot(hb, wa_ref[...], preferred_element_type=F32)
    pb_ref[...] = jnp.dot(hb, wb_ref[...], preferred_element_type=F32)
    pc_ref[...] = jnp.dot(hb, wc_ref[...], preferred_element_type=F32)


def _inproj(x, g, wa, wb, wc):
    n, d = x.shape
    tm = min(TOK_TILE, n)
    const = lambda i: (0, 0)
    row = lambda i: (i, 0)
    widths = (wa.shape[1], wb.shape[1], wc.shape[1])
    return pl.pallas_call(
        _inproj_body,
        grid=(n // tm,),
        in_specs=[pl.BlockSpec((tm, d), row), pl.BlockSpec((1, d), const)]
                 + [pl.BlockSpec((d, w), const) for w in widths],
        out_specs=[pl.BlockSpec((tm, w), row) for w in widths],
        out_shape=[jax.ShapeDtypeStruct((n, w), F32) for w in widths],
        compiler_params=_cparams("parallel"),
        name="inproj",
    )(x, g, wa, wb, wc)


def _memkv_body(m_ref, g_ref, wk_ref, wv_ref, k_ref, v_ref):
    hb = _rms(m_ref[...], g_ref[...]).astype(BF16)
    k_ref[...] = jnp.dot(hb, wk_ref[...], preferred_element_type=F32)
    v_ref[...] = jnp.dot(hb, wv_ref[...], preferred_element_type=F32)


def _memkv(mem, g, wk, wv):
    n, d = mem.shape
    tm = min(TOK_TILE, n)
    const = lambda i: (0, 0)
    row = lambda i: (i, 0)
    return pl.pallas_call(
        _memkv_body,
        grid=(n // tm,),
        in_specs=[pl.BlockSpec((tm, d), row), pl.BlockSpec((1, d), const),
                  pl.BlockSpec((d, d), const), pl.BlockSpec((d, d), const)],
        out_specs=[pl.BlockSpec((tm, d), row), pl.BlockSpec((tm, d), row)],
        out_shape=[jax.ShapeDtypeStruct((n, d), F32)] * 2,
        compiler_params=_cparams("parallel"),
        name="memkv",
    )(mem, g, wk, wv)


def _hgrn_body(pa_ref, loglb_ref, log1mlb_ref, gn_ref, m0_ref, oa_ref, mout_ref, m_scr, *, tile):
    c = pl.program_id(1)
    blk = HGRN_BLOCK
    nb = tile // blk

    @pl.when(c == 0)
    def _():
        m_scr[...] = m0_ref[0]

    x = pa_ref[0]
    q = x[:, 0:A_KW]
    f = x[:, A_KW:2 * A_KW]
    v = x[:, 2 * A_KW:2 * A_KW + A_W]
    g = x[:, 2 * A_KW + A_W:A_COLS]
    logf = _logaddexp(loglb_ref[...], log1mlb_ref[...] - _softplus(-f))
    ka = 1.0 - jnp.exp(logf)
    qa = _silu(q) * (A_DK ** -0.5)
    ti = _iota2((tile, tile), 0)
    si = _iota2((tile, tile), 1)
    same_blk = (ti // blk) == (si // blk)
    b = _mm_exact_lhs((same_blk & (ti >= si)).astype(BF16), logf)
    b_tot = jnp.broadcast_to(b.reshape(nb, blk, A_KW)[:, blk - 1:blk, :],
                             (nb, blk, A_KW)).reshape(tile, A_KW)
    q_t = qa * jnp.exp(b)
    k_hat = ka * jnp.exp(b_tot - b)
    d_blk = jnp.exp(b_tot)
    head_ones = _block_ones(A_KW, A_DK)
    head_ones_b = head_ones.astype(BF16)

    hb = blk // 2
    halves = lambda a: (lambda a4: (a4[:, 0], a4[:, 1]))(a.reshape(nb, 2, hb, A_KW))
    b_h, ka_h, qa_h, v_h = halves(b * LOG2E), halves(ka), halves(qa), halves(v)
    src_row = lambda a_h, s: a_h[s // hb][:, s % hb:s % hb + 1, :]
    row_h = lax.broadcasted_iota(jnp.int32, (nb, hb, A_KW), 1)
    slab_keys, slabs = [], []
    for s in range(blk):
        for half in range(s // hb, 2):
            z = qa_h[half] * src_row(ka_h, s) * jnp.exp2(jnp.minimum(b_h[half] - src_row(b_h, s), 0.0))
            if half == s // hb:
                z = jnp.where(row_h >= s % hb, z, 0.0)
            slab_keys.append((s, half))
            slabs.append(z.reshape(nb * hb, A_KW))
    att = _mm(jnp.concatenate(slabs, axis=0), head_ones_b)
    o_h = [jnp.zeros((nb, hb, A_W), F32), jnp.zeros((nb, hb, A_W), F32)]
    for i, (s, half) in enumerate(slab_keys):
        part = att[i * nb * hb:(i + 1) * nb * hb, :].reshape(nb, hb, A_W)
        o_h[half] = o_h[half] + part * src_row(v_h, s)
    o3 = jnp.stack(o_h, axis=1)

    rows = lambda a, j: a[j * blk:(j + 1) * blk, :]
    outer = [jnp.where(head_ones, _mm_tn(rows(v, j), rows(k_hat, j)), 0.0) for j in range(nb)]
    m = m_scr[...]
    states = []
    for j in range(nb):
        states.append(m)
        m = m * d_blk[j * blk:j * blk + 1, :] + outer[j]
    m_scr[...] = m
    o = jnp.concatenate([_mm_nt(rows(q_t, j), states[j]) for j in range(nb)], axis=0)
    o = o + o3.reshape(tile, A_W)
    ms = _mm(o * o, head_ones_b) * (1.0 / A_DV)
    oa_ref[0] = o * lax.rsqrt(ms + EPS) * gn_ref[...] * _silu(g)

    @pl.when(c == pl.num_programs(1) - 1)
    def _():
        mout_ref[0] = m_scr[...]


def _hgrn(pa, log_lb, log1m_lb, gn, m0):
    bsz, t, _ = pa.shape
    tile = min(HGRN_TILE, t)
    const = lambda b, c: (0, 0)
    return pl.pallas_call(
        functools.partial(_hgrn_body, tile=tile),
        grid=(bsz, t // tile),
        in_specs=[pl.BlockSpec((1, tile, A_COLS), lambda b, c: (b, c, 0)),
                  pl.BlockSpec((1, A_KW), const), pl.BlockSpec((1, A_KW), const),
                  pl.BlockSpec((1, A_W), const),
                  pl.BlockSpec((1, A_W, A_KW), lambda b, c: (b, 0, 0))],
        out_specs=[pl.BlockSpec((1, tile, A_W), lambda b, c: (b, c, 0)),
                   pl.BlockSpec((1, A_W, A_KW), lambda b, c: (b, 0, 0))],
        out_shape=[jax.ShapeDtypeStruct((bsz, t, A_W), F32),
                   jax.ShapeDtypeStruct((bsz, A_W, A_KW), F32)],
        scratch_shapes=[pltpu.VMEM((A_W, A_KW), F32)],
        compiler_params=_cparams("parallel", "arbitrary"),
        name="hgrn",
    )(pa, log_lb, log1m_lb, gn, m0)


def _ret_body(lgs_ref, pb_ref, cos_ref, sin_ref, lg_ref, m0_ref, ob_ref, mout_ref, m_scr, *, tile):
    c = pl.program_id(1)

    @pl.when(c == 0)
    def _():
        m_scr[...] = m0_ref[0]

    x = pb_ref[0]
    qk = x[:, 0:2 * B_KW]
    v = x[:, 2 * B_KW:2 * B_KW + B_W]
    g = x[:, 2 * B_KW + B_W:B_COLS]
    half = B_DK // 2
    first_half = (_iota2(qk.shape, 1) % B_DK) < half
    swapped = jnp.where(first_half, -pltpu.roll(qk, 2 * B_KW - half, 1), pltpu.roll(qk, half, 1))
    qk = qk * cos_ref[...] + swapped * sin_ref[...]
    q = qk[:, 0:B_KW]
    k = qk[:, B_KW:2 * B_KW] * (B_DK ** -0.5)

    lg_k = lg_ref[0:1, 0:B_KW]
    lg_v = lg_ref[1:2, :]
    j_k = _iota2((tile, B_KW), 0).astype(F32)
    j_v = _iota2((tile, B_W), 0).astype(F32)
    m = m_scr[...]
    cross = _mm_nt(q * jnp.exp(lg_k * (j_k + 1.0)), m)
    rel = (_iota2((tile, tile), 0) - _iota2((tile, tile), 1)).astype(F32)
    outs = []
    for h in range(B_HEADS):
        lg_h = lgs_ref[h]
        dmat = jnp.where(rel >= 0.0, jnp.exp(lg_h * jnp.maximum(rel, 0.0)), 0.0)
        s = _mm_nt(q[:, h * B_DK:(h + 1) * B_DK], k[:, h * B_DK:(h + 1) * B_DK]) * dmat
        outs.append(_mm(s, v[:, h * B_DV:(h + 1) * B_DV]))
    o = jnp.concatenate(outs, axis=-1) + cross
    mu = _seg_sum(o, B_DV) * (1.0 / B_DV)
    oc = o - mu
    var = _seg_sum(oc * oc, B_DV) * (1.0 / B_DV)
    ob_ref[0] = oc * lax.rsqrt(var + EPS) * _silu(g)
    v_w = v * jnp.exp(lg_v * (tile - 1.0 - j_v))
    head_mask = (_iota2((B_W, B_KW), 0) // B_DV) == (_iota2((B_W, B_KW), 1) // B_DK)
    m_scr[...] = m * jnp.exp(lg_k * float(tile)) + jnp.where(head_mask, _mm_tn(v_w, k), 0.0)

    @pl.when(c == pl.num_programs(1) - 1)
    def _():
        mout_ref[0] = m_scr[...]


def _retention(log_gamma, pb, cos_t, sin_t, lg_rows, m0):
    bsz, t, _ = pb.shape
    tile = min(RET_TILE, t)
    const = lambda b, c: (0, 0)
    return pl.pallas_call(
        functools.partial(_ret_body, tile=tile),
        grid=(bsz, t // tile),
        in_specs=[pl.BlockSpec(memory_space=pltpu.SMEM),
                  pl.BlockSpec((1, tile, B_COLS), lambda b, c: (b, c, 0)),
                  pl.BlockSpec((tile, 2 * B_KW), lambda b, c: (c, 0)),
                  pl.BlockSpec((tile, 2 * B_KW), lambda b, c: (c, 0)),
                  pl.BlockSpec((2, B_W), const),
                  pl.BlockSpec((1, B_W, B_KW), lambda b, c: (b, 0, 0))],
        out_specs=[pl.BlockSpec((1, tile, B_W), lambda b, c: (b, c, 0)),
                   pl.BlockSpec((1, B_W, B_KW), lambda b, c: (b, 0, 0))],
        out_shape=[jax.ShapeDtypeStruct((bsz, t, B_W), F32),
                   jax.ShapeDtypeStruct((bsz, B_W, B_KW), F32)],
        scratch_shapes=[pltpu.VMEM((B_W, B_KW), F32)],
        compiler_params=_cparams("parallel", "arbitrary"),
        name="retention",
    )(log_gamma, pb, cos_t, sin_t, lg_rows, m0)


def _rwkv_body(pc_ref, sh0_ref, s0_ref, mu_ref, vec_ref, w2_ref, a2_ref, g2_ref,
               yc_ref, sout_ref, shout_ref, s_scr, sh_scr, *, nchunk):
    c = pl.program_id(1)
    L = RWKV_CHUNK
    tl = nchunk * L
    npair = C_HEADS // 2

    @pl.when(c == 0)
    def _():
        s_scr[...] = s0_ref[0]
        sh_scr[...] = sh0_ref[0]

    p = pc_ref[0]
    row = _iota2((tl, C_COLS), 0)
    prev = jnp.where(row == 0, sh_scr[...], pltpu.roll(p, 1, 0))
    xs = p + mu_ref[...] * (prev - p)
    sh_scr[...] = p[tl - 1:tl, :]

    r = xs[:, 0:C_W]
    k = xs[:, C_W:2 * C_W]
    v = xs[:, 2 * C_W:3 * C_W]
    lora = xs[:, 3 * C_W:C_COLS]
    w0, a0, k_k, k_a, r_k, ln_w, ln_b = (vec_ref[i:i + 1, :] for i in range(7))
    w = -_softplus(-(w0 + _mm(jnp.tanh(lora), w2_ref[...]))) - 0.5
    lw = -jnp.exp(w)
    a = _sigmoid(a0 + _mm(lora, a2_ref[...]))
    gate = _mm(_sigmoid(lora), g2_ref[...])
    kk = k * k_k
    kk = kk / jnp.maximum(jnp.sqrt(_seg_sum(kk * kk, C_DH)), 1e-6)
    kc = k * (1.0 + (a - 1.0) * k_a)
    beta = kk * a

    ti = _iota2((tl, tl), 0)
    si = _iota2((tl, tl), 1)
    same_chunk = (ti // L) == (si // L)
    b = _mm_exact_lhs((same_chunk & (ti >= si)).astype(BF16), lw)
    b_tot = jnp.broadcast_to(b.reshape(nchunk, L, C_W)[:, L - 1:L, :],
                             (nchunk, L, C_W)).reshape(tl, C_W)
    e_nb = jnp.exp(-b)
    e_tail = jnp.exp(b_tot - b)
    kap_t = kk * jnp.exp(b - lw)
    r_t = r * jnp.exp(b)
    beta_t = beta * e_nb
    k_t = kc * e_nb
    beta_h = beta * e_tail
    k_h = kc * e_tail
    p_last = jnp.exp(b_tot)

    gch = min(RWKV_GROUP_CHUNKS, nchunk)
    nblk = 2 * gch
    wcat = L * nblk
    t_cat = _iota2((L, wcat), 0)
    s_cat = _iota2((L, wcat), 1) % L
    blk_cat = _iota2((L, wcat), 1) // L
    strict_cat = t_cat > s_cat
    incl_cat = t_cat >= s_cat
    eye_cat = (t_cat == s_cat).astype(F32)
    head0 = (_iota2((L, LANES), 1) // C_DH) == 0
    hmask = [head0, jnp.logical_not(head0)]
    pair_diag = _block_ones(LANES, C_DH)
    zeros_p = jnp.zeros((L, LANES), BF16)

    cut = lambda x, ci, pr: x[ci * L:(ci + 1) * L, pr * LANES:(pr + 1) * LANES]
    items = [(ci, pr) for ci in range(nchunk) for pr in range(npair)]
    groups = [(pr, cg) for cg in range(nchunk // gch) for pr in range(npair)]
    members = lambda cg: range(cg * gch, (cg + 1) * gch)

    def block_diag(x):
        xb = x.astype(BF16)
        return jnp.concatenate([jnp.where(blk_cat == j, xb, jnp.zeros_like(xb)) for j in range(nblk)], axis=0)

    def placed(ci, parts):
        width = len(parts) * LANES
        body = parts[0] if len(parts) == 1 else jnp.concatenate(parts, axis=1)
        secs = [body if cj == ci % gch else jnp.zeros((L, width), BF16) for cj in range(gch)]
        return secs[0] if gch == 1 else jnp.concatenate(secs, axis=1)

    def head_part(x, hh):
        return jnp.where(hmask[hh], x.astype(BF16), zeros_p)

    a_cat, b_cat, c_cat, e_cat = {}, {}, {}, {}
    gram = {}
    for ci, pr in items:
        bt, kt = cut(beta_t, ci, pr), cut(k_t, ci, pr)
        for hh in range(2):
            lhs = jnp.concatenate([jnp.where(hmask[hh], cut(kap_t, ci, pr), 0.0),
                                   jnp.where(hmask[hh], cut(r_t, ci, pr), 0.0)], axis=0)
            rhs = jnp.concatenate([bt, kt] if hh == 0 else [kt, bt], axis=0)
            gram[ci, pr, hh] = _mm_nt(lhs, rhs)
    top = lambda g0, g1: jnp.where(head0, g0[0:L], g1[0:L])
    bot = lambda g0, g1: jnp.where(head0, g0[L:2 * L], g1[L:2 * L])
    join = lambda xs: xs[0] if gch == 1 else jnp.concatenate(xs, axis=1)
    for g in groups:
        pr, cg = g
        a_cat[g] = jnp.where(strict_cat, join([top(gram[ci, pr, 0], gram[ci, pr, 1]) for ci in members(cg)]), 0.0)
        b_cat[g] = jnp.where(strict_cat, join([top(gram[ci, pr, 1], gram[ci, pr, 0]) for ci in members(cg)]), 0.0)
        c_cat[g] = jnp.where(incl_cat, join([bot(gram[ci, pr, 0], gram[ci, pr, 1]) for ci in members(cg)]), 0.0)
        e_cat[g] = jnp.where(incl_cat, join([bot(gram[ci, pr, 1], gram[ci, pr, 0]) for ci in members(cg)]), 0.0)
    order_ac = (0, 1)
    order_be = (1, 0)

    n_pow = {g: -a_cat[g] for g in groups}
    t_inv = {g: eye_cat + n_pow[g] for g in groups}
    for _ in range(L.bit_length() - 2):
        n_pow = {g: jnp.dot(n_pow[g].astype(BF16), block_diag(n_pow[g]), preferred_element_type=F32)
                 for g in groups}
        upd = {g: jnp.dot(n_pow[g].astype(BF16), block_diag(t_inv[g]), preferred_element_type=F32)
               for g in groups}
        t_inv = {g: t_inv[g] + upd[g] for g in groups}

    def stacked(cg, order, part_fn):
        return jnp.concatenate([placed(ci, part_fn(ci, hh)) for ci in members(cg) for hh in order], axis=0)

    sec = lambda x, ci, width: x[:, (ci % gch) * width:(ci % gch + 1) * width]
    bv = {(pr, cg): jnp.dot(b_cat[pr, cg].astype(BF16),
                            stacked(cg, order_be, lambda ci, hh: [head_part(cut(v, ci, pr), hh)]),
                            preferred_element_type=F32) for pr, cg in groups}
    tr = {(pr, cg): jnp.dot(t_inv[pr, cg].astype(BF16),
                            stacked(cg, order_ac, lambda ci, hh: [head_part(sec(bv[pr, cg], ci, LANES), hh),
                                                                  head_part(cut(kap_t, ci, pr), hh)]),
                            preferred_element_type=F32) for pr, cg in groups}
    u0 = {(ci, pr): -sec(tr[pr, ci // gch], ci, 2 * LANES)[:, 0:LANES] for ci, pr in items}
    wm = {(ci, pr): sec(tr[pr, ci // gch], ci, 2 * LANES)[:, LANES:2 * LANES] for ci, pr in items}
    ce = {(pr, cg): jnp.dot(
        jnp.concatenate([c_cat[pr, cg], e_cat[pr, cg]], axis=1).astype(BF16),
        jnp.concatenate(
            [stacked(cg, order_ac, lambda ci, hh: [head_part(u0[ci, pr], hh), head_part(wm[ci, pr], hh)]),
             stacked(cg, order_be, lambda ci, hh: [head_part(cut(v, ci, pr), hh), zeros_p])], axis=0),
        preferred_element_type=F32) for pr, cg in groups}
    y0 = {(ci, pr): sec(ce[pr, ci // gch], ci, 2 * LANES)[:, 0:LANES] for ci, pr in items}
    r_hat = {(ci, pr): cut(r_t, ci, pr) - sec(ce[pr, ci // gch], ci, 2 * LANES)[:, LANES:2 * LANES]
             for ci, pr in items}
    q_m = {it: -jnp.where(pair_diag, _mm_tn(wm[it], cut(beta_h, *it)), 0.0) for it in items}
    z_m = {it: jnp.where(pair_diag,
                         _mm_tn(jnp.concatenate([u0[it], cut(v, *it)], axis=0),
                                jnp.concatenate([cut(beta_h, *it), cut(k_h, *it)], axis=0)), 0.0)
           for it in items}

    s_mat = [s_scr[pr] for pr in range(npair)]
    y_rows = []
    for ci in range(nchunk):
        y_dep = [_mm_nt(r_hat[ci, pr], s_mat[pr]) for pr in range(npair)]
        s_q = [_mm(s_mat[pr], q_m[ci, pr]) for pr in range(npair)]
        s_mat = [s_mat[pr] * p_last[ci * L:ci * L + 1, pr * LANES:(pr + 1) * LANES] + s_q[pr] + z_m[ci, pr]
                 for pr in range(npair)]
        y_rows.append(jnp.concatenate([y_dep[pr] + y0[ci, pr] for pr in range(npair)], axis=-1))
    for pr in range(npair):
        s_scr[pr] = s_mat[pr]

    y = y_rows[0] if nchunk == 1 else jnp.concatenate(y_rows, axis=0)
    mu_y = _seg_sum(y, C_DH) * (1.0 / C_DH)
    yc = y - mu_y
    var = _seg_sum(yc * yc, C_DH) * (1.0 / C_DH)
    yn = yc * lax.rsqrt(var + RWKV_GN_EPS) * ln_w + ln_b
    bonus = _seg_sum(r * kc * r_k, C_DH) * v
    yc_ref[0] = (yn + bonus) * gate

    @pl.when(c == pl.num_programs(1) - 1)
    def _():
        sout_ref[0] = s_scr[...]
        shout_ref[0] = sh_scr[...]


def _rwkv(pc, sh0, s0, mu, vecs, w2p, a2p, g2p):
    bsz, t, _ = pc.shape
    L = min(RWKV_TILE, t)
    npair = C_HEADS // 2
    const = lambda b, c: (0, 0)
    return pl.pallas_call(
        functools.partial(_rwkv_body, nchunk=L // RWKV_CHUNK),
        grid=(bsz, t // L),
        in_specs=[pl.BlockSpec((1, L, C_COLS), lambda b, c: (b, c, 0)),
                  pl.BlockSpec((1, 1, C_COLS), lambda b, c: (b, 0, 0)),
                  pl.BlockSpec((1, npair, LANES, LANES), lambda b, c: (b, 0, 0, 0)),
                  pl.BlockSpec((1, C_COLS), const),
                  pl.BlockSpec((8, C_W), const),
                  pl.BlockSpec((C_LORA, C_W), const),
                  pl.BlockSpec((C_LORA, C_W), const),
                  pl.BlockSpec((C_LORA, C_W), const)],
        out_specs=[pl.BlockSpec((1, L, C_W), lambda b, c: (b, c, 0)),
                   pl.BlockSpec((1, npair, LANES, LANES), lambda b, c: (b, 0, 0, 0)),
                   pl.BlockSpec((1, 1, C_COLS), lambda b, c: (b, 0, 0))],
        out_shape=[jax.ShapeDtypeStruct((bsz, t, C_W), F32),
                   jax.ShapeDtypeStruct((bsz, npair, LANES, LANES), F32),
                   jax.ShapeDtypeStruct((bsz, 1, C_COLS), F32)],
        scratch_shapes=[pltpu.VMEM((npair, LANES, LANES), F32), pltpu.VMEM((1, C_COLS), F32)],
        compiler_params=_cparams("parallel", "arbitrary"),
        name="rwkv",
    )(pc, sh0, s0, mu, vecs, w2p, a2p, g2p)


def _outattn_body(x_ref, oa_ref, ob_ref, yc_ref, woa_ref, wob_ref, woc_ref, g_ref, wq_ref, wo_ref,
                  mk_ref, mv_ref, o_ref):
    x = (x_ref[0] + _mm(oa_ref[0], woa_ref[...]) + _mm(ob_ref[0], wob_ref[...])
         + _mm(yc_ref[0], woc_ref[...]))
    q = _mm(_rms(x, g_ref[...]), wq_ref[...])
    d = x.shape[-1]
    hd = d // MEM_HEADS
    outs = []
    for h in range(MEM_HEADS):
        cols = slice(h * hd, (h + 1) * hd)
        s = _mm_nt(q[:, cols], mk_ref[0, :, cols]) * (hd ** -0.5)
        s = s - jnp.max(s, axis=-1, keepdims=True)
        e = jnp.exp(s)
        pr = e / jnp.sum(e, axis=-1, keepdims=True)
        outs.append(_mm(pr, mv_ref[0, :, cols]))
    o_ref[0] = x + _mm(jnp.concatenate(outs, axis=-1), wo_ref[...])


def _outattn(x, oa, ob, yc, woa, wob, woc, g, wq, wo, mk, mv):
    bsz, t, d = x.shape
    tm = min(TOK_TILE, t)
    n_mem = mk.shape[1]
    const = lambda b, i: (0, 0)
    tok = lambda w: pl.BlockSpec((1, tm, w), lambda b, i: (b, i, 0))
    full = lambda a: pl.BlockSpec(a.shape, const)
    mem = pl.BlockSpec((1, n_mem, d), lambda b, i: (b, 0, 0))
    return pl.pallas_call(
        _outattn_body,
        grid=(bsz, t // tm),
        in_specs=[tok(d), tok(A_W), tok(B_W), tok(C_W), full(woa), full(wob), full(woc), full(g),
                  full(wq), full(wo), mem, mem],
        out_specs=tok(d),
        out_shape=jax.ShapeDtypeStruct((bsz, t, d), F32),
        compiler_params=_cparams("parallel", "parallel"),
        name="outattn",
    )(x, oa, ob, yc, woa, wob, woc, g, wq, wo, mk, mv)


def _blockdiag_t(s):
    bsz, h, dk, dv = s.shape
    eye = jnp.eye(h, dtype=s.dtype)
    return jnp.einsum('bhkv,hg->bhvgk', s, eye).reshape(bsz, h * dv, h * dk)


def _unblockdiag_t(m, h, dk, dv):
    bsz = m.shape[0]
    m5 = m.reshape(bsz, h, dv, h, dk)
    return jnp.stack([m5[:, i, :, i, :] for i in range(h)], axis=1).transpose(0, 1, 3, 2)


def _rwkv_pack(s):
    bsz = s.shape[0]
    s4 = s.reshape(bsz, C_HEADS // 2, 2, C_DH, C_DH)
    eye = jnp.eye(2, dtype=s.dtype)
    return jnp.einsum('bphvk,hg->bphvgk', s4, eye).reshape(bsz, C_HEADS // 2, LANES, LANES)


def _rwkv_unpack(m):
    bsz = m.shape[0]
    m6 = m.reshape(bsz, C_HEADS // 2, 2, C_DH, 2, C_DH)
    return jnp.stack([m6[:, :, i, :, i, :] for i in range(2)], axis=2).reshape(bsz, C_HEADS, C_DH, C_DH)


def _rope_tables(pos):
    half = B_DK // 2
    inv = ROPE_BASE ** (-jnp.arange(half, dtype=F32) / half)
    ang = pos.astype(F32)[:, None] * inv[None, :]
    cos = jnp.tile(jnp.cos(ang), (1, 2 * 2 * B_HEADS))
    sin = jnp.tile(jnp.sin(ang), (1, 2 * 2 * B_HEADS))
    return cos, sin


def _prep_layer(l, W, lbs, log_gamma):
    row = lambda a: a.reshape(1, -1).astype(F32)
    w_in = W['w_in'][l]
    pad_rows = lambda m, off: jnp.zeros((C_LORA, C_W), F32).at[off:off + m.shape[0]].set(m).astype(BF16)
    lb = lbs[l]
    zero = jnp.zeros((C_W,), F32)
    vecs = jnp.stack([W['rwkv_w0'][l], W['rwkv_a0'][l], W['rwkv_k_k'][l], W['rwkv_k_a'][l],
                      W['rwkv_r_k'][l].reshape(-1), W['rwkv_ln_w'][l], W['rwkv_ln_b'][l], zero])
    lg_rows = jnp.stack([jnp.pad(jnp.repeat(log_gamma, B_DK), (0, B_W - B_KW)),
                         jnp.repeat(log_gamma, B_DV)])
    w_out = W['w_out'][l].astype(BF16)
    return dict(
        n1=row(W['norm_ffn1'][l]), g1=W['ffn1_wg'][l].astype(BF16), u1=W['ffn1_wu'][l].astype(BF16),
        d1=W['ffn1_wd'][l].astype(BF16),
        n2=row(W['norm_ffn2'][l]), g2=W['ffn2_wg'][l].astype(BF16), u2=W['ffn2_wu'][l].astype(BF16),
        d2=W['ffn2_wd'][l].astype(BF16),
        nmix=row(W['norm_mix'][l]),
        wa=w_in[:, :A_COLS].astype(BF16), wb=w_in[:, A_COLS:A_COLS + B_COLS].astype(BF16),
        wc=w_in[:, A_COLS + B_COLS:].astype(BF16),
        log_lb=row(jnp.log(jnp.maximum(lb, LB_FLOOR))), log1m_lb=row(jnp.log1p(-lb)),
        gn=row(jnp.tile(W['hgrn_norm'][l], A_HEADS)),
        lg_rows=lg_rows, log_gamma=log_gamma,
        mu=row(W['rwkv_mu'][l]), vecs=vecs,
        w2p=pad_rows(W['rwkv_w2'][l], 0), a2p=pad_rows(W['rwkv_a2'][l], C_DECAY_RANK),
        g2p=pad_rows(W['rwkv_g2'][l], C_DECAY_RANK + C_AAA_RANK),
        woa=w_out[:A_W], wob=w_out[A_W:A_W + B_W], woc=w_out[A_W + B_W:],
        ncross=row(W['norm_cross'][l]), wq=W['wq_x'][l].astype(BF16), wo=W['wo_x'][l].astype(BF16),
    )


def _trunk(x, pos, mem_k, mem_v, st_a, st_b, st_c, st_sh, layers, final_g):
    bsz, t, d = x.shape
    n = bsz * t
    depth = len(layers)
    cos_t, sin_t = _rope_tables(pos)
    new_a, new_b, new_c, new_sh = [], [], [], []
    x2 = x.reshape(n, d)
    for l, p in enumerate(layers):
        x2 = _ffn(x2, p['n1'], p['g1'], p['u1'], p['d1'], final_g, False)
        pa, pb, pc = _inproj(x2, p['nmix'], p['wa'], p['wb'], p['wc'])
        oa, ma = _hgrn(pa.reshape(bsz, t, A_COLS), p['log_lb'], p['log1m_lb'], p['gn'],
                       _blockdiag_t(st_a[l]))
        ob, mb = _retention(p['log_gamma'], pb.reshape(bsz, t, B_COLS), cos_t, sin_t, p['lg_rows'],
                            _blockdiag_t(st_b[l]))
        yc, sc, sh = _rwkv(pc.reshape(bsz, t, C_COLS), st_sh[l], _rwkv_pack(st_c[l]), p['mu'],
                           p['vecs'], p['w2p'], p['a2p'], p['g2p'])
        x3 = _outattn(x2.reshape(bsz, t, d), oa, ob, yc, p['woa'], p['wob'], p['woc'], p['ncross'],
                      p['wq'], p['wo'], mem_k[l].reshape(bsz, -1, d), mem_v[l].reshape(bsz, -1, d))
        x2 = _ffn(x3.reshape(n, d), p['n2'], p['g2'], p['u2'], p['d2'], final_g, l == depth - 1)
        new_a.append(_unblockdiag_t(ma, A_HEADS, A_DK, A_DV))
        new_b.append(_unblockdiag_t(mb, B_HEADS, B_DK, B_DV))
        new_c.append(_rwkv_unpack(sc))
        new_sh.append(sh)
    return (x2.reshape(bsz, t, d), jnp.stack(new_a), jnp.stack(new_b), jnp.stack(new_c),
            jnp.stack(new_sh))


def kernel(x_prompt, x_sample, mem_prompt, cache_mem_k, cache_mem_v, state_hgrn, state_ret, state_rwkv, state_rwkv_shift, norm_ffn1, ffn1_wg, ffn1_wu, ffn1_wd, norm_mix, w_in, hgrn_lb_param, hgrn_norm, rwkv_mu, rwkv_w0, rwkv_w2, rwkv_a0, rwkv_a2, rwkv_g2, rwkv_k_k, rwkv_k_a, rwkv_r_k, rwkv_ln_w, rwkv_ln_b, w_out, norm_cross, norm_mem, wq_x, wk_x, wv_x, wo_x, norm_ffn2, ffn2_wg, ffn2_wu, ffn2_wd, final_norm):
    W = dict(norm_ffn1=norm_ffn1, ffn1_wg=ffn1_wg, ffn1_wu=ffn1_wu, ffn1_wd=ffn1_wd, norm_mix=norm_mix,
             w_in=w_in, hgrn_norm=hgrn_norm, rwkv_mu=rwkv_mu, rwkv_w0=rwkv_w0, rwkv_w2=rwkv_w2,
             rwkv_a0=rwkv_a0, rwkv_a2=rwkv_a2, rwkv_g2=rwkv_g2, rwkv_k_k=rwkv_k_k, rwkv_k_a=rwkv_k_a,
             rwkv_r_k=rwkv_r_k, rwkv_ln_w=rwkv_ln_w, rwkv_ln_b=rwkv_ln_b, w_out=w_out,
             norm_cross=norm_cross, wq_x=wq_x, wo_x=wo_x, norm_ffn2=norm_ffn2, ffn2_wg=ffn2_wg,
             ffn2_wu=ffn2_wu, ffn2_wd=ffn2_wd)
    depth = w_in.shape[0]
    d = x_prompt.shape[-1]
    sm = jax.nn.softmax(hgrn_lb_param.astype(F32), axis=0)
    lbs = jnp.cumsum(sm, axis=0) - sm[0:1]
    log_gamma = jnp.log1p(-jnp.exp2(-5.0 - jnp.arange(B_HEADS, dtype=F32)))
    layers = [_prep_layer(l, W, lbs, log_gamma) for l in range(depth)]
    final_g = final_norm.reshape(1, -1).astype(F32)

    bp, tp, _ = x_prompt.shape
    n_mem = mem_prompt.shape[1]
    hd = d // MEM_HEADS
    mem2 = mem_prompt.reshape(bp * n_mem, d)
    mks, mvs = [], []
    for l in range(depth):
        mk, mv = _memkv(mem2, norm_mem[l].reshape(1, -1), wk_x[l].astype(BF16), wv_x[l].astype(BF16))
        mks.append(mk.reshape(bp, n_mem, MEM_HEADS, hd))
        mvs.append(mv.reshape(bp, n_mem, MEM_HEADS, hd))
    p_mem_k = jnp.stack(mks)
    p_mem_v = jnp.stack(mvs)

    zeros = lambda *s: jnp.zeros((depth, bp) + s, F32)
    y_prompt, p_hgrn, p_ret, p_rwkv, p_shift = _trunk(
        x_prompt, jnp.arange(tp, dtype=jnp.int32), p_mem_k, p_mem_v,
        zeros(A_HEADS, A_DK, A_DV), zeros(B_HEADS, B_DK, B_DV), zeros(C_HEADS, C_DH, C_DH),
        zeros(1, C_COLS), layers, final_g)

    ts = x_sample.shape[1]
    y_sample, s_hgrn, s_ret, s_rwkv, s_shift = _trunk(
        x_sample, PAST_LEN + jnp.arange(ts, dtype=jnp.int32), cache_mem_k, cache_mem_v,
        state_hgrn, state_ret, state_rwkv, state_rwkv_shift, layers, final_g)

    return (y_prompt, y_sample, p_hgrn, p_ret, p_rwkv, p_shift, p_mem_k, p_mem_v,
            s_hgrn, s_ret, s_rwkv, s_shift)
```

```python
import functools

import jax
import jax.numpy as jnp
from jax import lax
from jax.experimental import pallas as pl
from jax.experimental.pallas import tpu as pltpu

F32 = jnp.float32
BF16 = jnp.bfloat16

EPS = 1e-6
LB_FLOOR = 1e-20
ROPE_BASE = 10000.0
PAST_LEN = 4096
MEM_HEADS = 4
A_HEADS, A_DK, A_DV = 4, 64, 64
A_KW = A_HEADS * A_DK
A_W = A_HEADS * A_DV
A_COLS = 2 * A_KW + 2 * A_W
B_HEADS, B_DK, B_DV = 6, 32, 64
B_KW = B_HEADS * B_DK
B_W = B_HEADS * B_DV
B_COLS = 2 * B_KW + 2 * B_W
C_HEADS, C_DH = 6, 64
C_W = C_HEADS * C_DH
C_DECAY_RANK, C_AAA_RANK, C_GATE_RANK = 32, 32, 64
C_LORA = C_DECAY_RANK + C_AAA_RANK + C_GATE_RANK
C_COLS = 3 * C_W + C_LORA
RWKV_GN_EPS = 64e-5
LOG2E = 1.4426950408889634

LANES = 128
VMEM_LIMIT = 56 * 1024 * 1024
FFN_TILE = 1024
FFN_FCHUNK = 256
TOK_TILE = 512
SUB_ROWS = 256
HGRN_BLOCK = 16
HGRN_TILE = 256
RET_TILE = 256
RWKV_CHUNK = 64
RWKV_TILE = 256
RWKV_GROUP_CHUNKS = 2


def _cparams(*sem):
    return pltpu.CompilerParams(dimension_semantics=sem, vmem_limit_bytes=VMEM_LIMIT)


def _mm(a, b):
    return jnp.dot(a.astype(BF16), b.astype(BF16), preferred_element_type=F32)


def _mm_nt(a, b):
    return lax.dot_general(a.astype(BF16), b.astype(BF16), (((1,), (1,)), ((), ())),
                           preferred_element_type=F32)


def _mm_tn(a, b):
    return lax.dot_general(a.astype(BF16), b.astype(BF16), (((0,), (0,)), ((), ())),
                           preferred_element_type=F32)


def _split3(x):
    hi = x.astype(BF16)
    r = x - hi.astype(F32)
    mid = r.astype(BF16)
    lo = (r - mid.astype(F32)).astype(BF16)
    return hi, mid, lo


def _mm_exact_lhs(c, x):
    hi, mid, lo = _split3(x)
    return (jnp.dot(c, hi, preferred_element_type=F32) + jnp.dot(c, mid, preferred_element_type=F32)
            + jnp.dot(c, lo, preferred_element_type=F32))


def _rms(x, g):
    return x * lax.rsqrt(jnp.mean(x * x, axis=-1, keepdims=True) + EPS) * g


def _sigmoid(x):
    return 0.5 * jnp.tanh(0.5 * x) + 0.5


def _silu(x):
    return x * _sigmoid(x)


def _softplus(x):
    return jnp.maximum(x, 0.0) + jnp.log(1.0 + jnp.exp(-jnp.abs(x)))


def _row_subtiles(rows):
    n = max(1, rows // SUB_ROWS)
    step = rows // n
    return [slice(i * step, (i + 1) * step) for i in range(n)]


def _iota2(shape, axis):
    return lax.broadcasted_iota(jnp.int32, shape, axis)


def _block_ones(n, blk):
    return (_iota2((n, n), 0) // blk) == (_iota2((n, n), 1) // blk)


def _seg_sum(x, blk):
    n = x.shape[-1]
    outs = []
    for lo in range(0, n, LANES):
        w = min(LANES, n - lo)
        outs.append(jnp.dot(x[:, lo:lo + w].astype(BF16), _block_ones(w, blk).astype(BF16),
                            preferred_element_type=F32))
    return outs[0] if len(outs) == 1 else jnp.concatenate(outs, axis=-1)


def _ffn_body(x_ref, g_ref, wg_ref, wu_ref, wd_ref, gf_ref, o_ref, *, apply_final_norm):
    x = x_ref[...]
    hb = _rms(x, g_ref[...]).astype(BF16)
    d_ff = wg_ref.shape[1]
    acc = jnp.zeros(x.shape, F32)
    for j in range(d_ff // FFN_FCHUNK):
        cols = slice(j * FFN_FCHUNK, (j + 1) * FFN_FCHUNK)
        a = jnp.dot(hb, wg_ref[:, cols], preferred_element_type=F32)
        u = jnp.dot(hb, wu_ref[:, cols], preferred_element_type=F32)
        t = (_silu(a) * u).astype(BF16)
        acc = acc + jnp.dot(t, wd_ref[cols, :], preferred_element_type=F32)
    y = x + 0.5 * acc
    if apply_final_norm:
        y = _rms(y, gf_ref[...])
    o_ref[...] = y


def _ffn(x, g, wg, wu, wd, gf, apply_final_norm):
    n, d = x.shape
    d_ff = wg.shape[1]
    tm = min(FFN_TILE, n)
    const = lambda i: (0, 0)
    resident = lambda shape: pl.BlockSpec(shape, const, pipeline_mode=pl.Buffered(1))
    return pl.pallas_call(
        functools.partial(_ffn_body, apply_final_norm=apply_final_norm),
        grid=(n // tm,),
        in_specs=[pl.BlockSpec((tm, d), lambda i: (i, 0)),
                  pl.BlockSpec((1, d), const),
                  resident((d, d_ff)),
                  resident((d, d_ff)),
                  resident((d_ff, d)),
                  pl.BlockSpec((1, d), const)],
        out_specs=pl.BlockSpec((tm, d), lambda i: (i, 0)),
        out_shape=jax.ShapeDtypeStruct((n, d), F32),
        compiler_params=_cparams("parallel"),
        name="ffn",
    )(x, g, wg, wu, wd, gf)


def _inproj_body(x_ref, g_ref, wa_ref, wb_ref, wc_ref, pa_ref, pb_ref, pc_ref):
    hb = _rms(x_ref[...], g_ref[...]).astype(BF16)
    pa_ref[...] = jnp.dot(hb, wa_ref[...], preferred_element_type=F32)
    pb_ref[...] = jnp.dot(hb, wb_ref[...], preferred_element_type=F32)
    pc_ref[...] = jnp.dot(hb, wc_ref[...], preferred_element_type=F32)


def _inproj(x, g, wa, wb, wc):
    n, d = x.shape
    tm = min(TOK_TILE, n)
    const = lambda i: (0, 0)
    row = lambda i: (i, 0)
    widths = (wa.shape[1], wb.shape[1], wc.shape[1])
    return pl.pallas_call(
        _inproj_body,
        grid=(n // tm,),
        in_specs=[pl.BlockSpec((tm, d), row), pl.BlockSpec((1, d), const)]
                 + [pl.BlockSpec((d, w), const) for w in widths],
        out_specs=[pl.BlockSpec((tm, w), row) for w in widths],
        out_shape=[jax.ShapeDtypeStruct((n, w), F32) for w in widths],
        compiler_params=_cparams("parallel"),
        name="inproj",
    )(x, g, wa, wb, wc)


def _memkv_body(m_ref, g_ref, wk_ref, wv_ref, k_ref, v_ref):
    hb = _rms(m_ref[...], g_ref[...]).astype(BF16)
    k_ref[...] = jnp.dot(hb, wk_ref[...], preferred_element_type=F32)
    v_ref[...] = jnp.dot(hb, wv_ref[...], preferred_element_type=F32)


def _memkv(mem, g, wk, wv):
    n, d = mem.shape
    tm = min(TOK_TILE, n)
    const = lambda i: (0, 0)
    row = lambda i: (i, 0)
    return pl.pallas_call(
        _memkv_body,
        grid=(n // tm,),
        in_specs=[pl.BlockSpec((tm, d), row), pl.BlockSpec((1, d), const),
                  pl.BlockSpec((d, d), const), pl.BlockSpec((d, d), const)],
        out_specs=[pl.BlockSpec((tm, d), row), pl.BlockSpec((tm, d), row)],
        out_shape=[jax.ShapeDtypeStruct((n, d), F32)] * 2,
        compiler_params=_cparams("parallel"),
        name="memkv",
    )(mem, g, wk, wv)


def _hgrn_body(pa_ref, lbf_ref, oml_ref, gn_ref, m0_ref, oa_ref, mout_ref, m_scr, *, tile):
    c = pl.program_id(1)
    blk = HGRN_BLOCK
    nb = tile // blk

    @pl.when(c == 0)
    def _():
        m_scr[...] = m0_ref[0]

    x = pa_ref[0]
    q = x[:, 0:A_KW]
    f = x[:, A_KW:2 * A_KW]
    v = x[:, 2 * A_KW:2 * A_KW + A_W]
    g = x[:, 2 * A_KW + A_W:A_COLS]
    gate_f = lbf_ref[...] + oml_ref[...] * _sigmoid(f)
    logf = jnp.log(gate_f)
    ka = 1.0 - gate_f
    qa = _silu(q) * (A_DK ** -0.5)
    ti = _iota2((tile, tile), 0)
    si = _iota2((tile, tile), 1)
    same_blk = (ti // blk) == (si // blk)
    b = _mm_exact_lhs((same_blk & (ti >= si)).astype(BF16), logf)
    b_tot = jnp.broadcast_to(b.reshape(nb, blk, A_KW)[:, blk - 1:blk, :],
                             (nb, blk, A_KW)).reshape(tile, A_KW)
    q_t = qa * jnp.exp(b)
    k_hat = ka * jnp.exp(b_tot - b)
    d_blk = jnp.exp(b_tot)
    head_ones = _block_ones(A_KW, A_DK)
    head_ones_b = head_ones.astype(BF16)

    hb = blk // 2
    halves = lambda a: (lambda a4: (a4[:, 0], a4[:, 1]))(a.reshape(nb, 2, hb, A_KW))
    b_h, ka_h, qa_h, v_h = halves(b * LOG2E), halves(ka), halves(qa), halves(v)
    src_row = lambda a_h, s: a_h[s // hb][:, s % hb:s % hb + 1, :]
    row_h = lax.broadcasted_iota(jnp.int32, (nb, hb, A_KW), 1)
    slab_keys, slabs = [], []
    for s in range(blk):
        for half in range(s // hb, 2):
            z = qa_h[half] * src_row(ka_h, s) * jnp.exp2(jnp.minimum(b_h[half] - src_row(b_h, s), 0.0))
            if half == s // hb:
                z = jnp.where(row_h >= s % hb, z, 0.0)
            slab_keys.append((s, half))
            slabs.append(z.reshape(nb * hb, A_KW))
    att = _mm(jnp.concatenate(slabs, axis=0), head_ones_b)
    o_h = [jnp.zeros((nb, hb, A_W), F32), jnp.zeros((nb, hb, A_W), F32)]
    for i, (s, half) in enumerate(slab_keys):
        part = att[i * nb * hb:(i + 1) * nb * hb, :].reshape(nb, hb, A_W)
        o_h[half] = o_h[half] + part * src_row(v_h, s)
    o3 = jnp.stack(o_h, axis=1)

    npair = A_HEADS // 2
    pair_diag = _block_ones(LANES, A_DK)
    part = lambda a, j, p: a[j * blk:(j + 1) * blk, p * LANES:(p + 1) * LANES]
    outer = [[jnp.where(pair_diag, _mm_tn(part(v, j, p), part(k_hat, j, p)), 0.0) for p in range(npair)]
             for j in range(nb)]
    m = [m_scr[p] for p in range(npair)]
    states = []
    for j in range(nb):
        states.append(m)
        m = [m[p] * d_blk[j * blk:j * blk + 1, p * LANES:(p + 1) * LANES] + outer[j][p] for p in range(npair)]
    for p in range(npair):
        m_scr[p] = m[p]
    o = jnp.concatenate(
        [jnp.concatenate([_mm_nt(part(q_t, j, p), states[j][p]) for p in range(npair)], axis=1)
         for j in range(nb)], axis=0)
    o = o + o3.reshape(tile, A_W)
    ms = _mm(o * o, head_ones_b) * (1.0 / A_DV)
    oa_ref[0] = o * lax.rsqrt(ms + EPS) * gn_ref[...] * _silu(g)

    @pl.when(c == pl.num_programs(1) - 1)
    def _():
        mout_ref[0] = m_scr[...]


def _hgrn(pa, lb_floor, one_minus_lb, gn, m0):
    bsz, t, _ = pa.shape
    tile = min(HGRN_TILE, t)
    npair = A_HEADS // 2
    const = lambda b, c: (0, 0)
    return pl.pallas_call(
        functools.partial(_hgrn_body, tile=tile),
        grid=(bsz, t // tile),
        in_specs=[pl.BlockSpec((1, tile, A_COLS), lambda b, c: (b, c, 0)),
                  pl.BlockSpec((1, A_KW), const), pl.BlockSpec((1, A_KW), const),
                  pl.BlockSpec((1, A_W), const),
                  pl.BlockSpec((1, npair, LANES, LANES), lambda b, c: (b, 0, 0, 0))],
        out_specs=[pl.BlockSpec((1, tile, A_W), lambda b, c: (b, c, 0)),
                   pl.BlockSpec((1, npair, LANES, LANES), lambda b, c: (b, 0, 0, 0))],
        out_shape=[jax.ShapeDtypeStruct((bsz, t, A_W), F32),
                   jax.ShapeDtypeStruct((bsz, npair, LANES, LANES), F32)],
        scratch_shapes=[pltpu.VMEM((npair, LANES, LANES), F32)],
        compiler_params=_cparams("parallel", "arbitrary"),
        name="hgrn",
    )(pa, lb_floor, one_minus_lb, gn, m0)


def _ret_body(lgs_ref, pb_ref, cos_ref, sin_ref, lg_ref, m0_ref, ob_ref, mout_ref, m_scr, d_scr, eq_scr, ev_scr,
              *, tile):
    c = pl.program_id(1)

    @pl.when(c == 0)
    def _():
        m_scr[...] = m0_ref[0]
        rel = (_iota2((tile, tile), 0) - _iota2((tile, tile), 1)).astype(F32)
        for h in range(B_HEADS):
            d_scr[h] = jnp.where(rel >= 0.0, jnp.exp(lgs_ref[h] * jnp.maximum(rel, 0.0)), 0.0)
        j_k = _iota2((tile, B_KW), 0).astype(F32)
        j_v = _iota2((tile, B_W), 0).astype(F32)
        eq_scr[...] = jnp.exp(lg_ref[0:1, 0:B_KW] * (j_k + 1.0))
        ev_scr[...] = jnp.exp(lg_ref[1:2, :] * (tile - 1.0 - j_v))

    x = pb_ref[0]
    qk = x[:, 0:2 * B_KW]
    v = x[:, 2 * B_KW:2 * B_KW + B_W]
    g = x[:, 2 * B_KW + B_W:B_COLS]
    half = B_DK // 2
    first_half = (_iota2(qk.shape, 1) % B_DK) < half
    swapped = jnp.where(first_half, -pltpu.roll(qk, 2 * B_KW - half, 1), pltpu.roll(qk, half, 1))
    qk = qk * cos_ref[...] + swapped * sin_ref[...]
    q = qk[:, 0:B_KW]
    k = qk[:, B_KW:2 * B_KW] * (B_DK ** -0.5)

    lg_k = lg_ref[0:1, 0:B_KW]
    m = m_scr[...]
    cross = _mm_nt(q * eq_scr[...], m)
    scores = [_mm_nt(q[:, h * B_DK:(h + 1) * B_DK], k[:, h * B_DK:(h + 1) * B_DK]) for h in range(B_HEADS)]
    outs = [_mm(scores[h] * d_scr[h], v[:, h * B_DV:(h + 1) * B_DV]) for h in range(B_HEADS)]
    o = jnp.concatenate(outs, axis=-1) + cross
    mu = _seg_sum(o, B_DV) * (1.0 / B_DV)
    oc = o - mu
    var = _seg_sum(oc * oc, B_DV) * (1.0 / B_DV)
    ob_ref[0] = oc * lax.rsqrt(var + EPS) * _silu(g)
    v_w = v * ev_scr[...]
    head_mask = (_iota2((B_W, B_KW), 0) // B_DV) == (_iota2((B_W, B_KW), 1) // B_DK)
    m_scr[...] = m * jnp.exp(lg_k * float(tile)) + jnp.where(head_mask, _mm_tn(v_w, k), 0.0)

    @pl.when(c == pl.num_programs(1) - 1)
    def _():
        mout_ref[0] = m_scr[...]


def _retention(log_gamma, pb, cos_t, sin_t, lg_rows, m0):
    bsz, t, _ = pb.shape
    tile = min(RET_TILE, t)
    const = lambda b, c: (0, 0)
    return pl.pallas_call(
        functools.partial(_ret_body, tile=tile),
        grid=(bsz, t // tile),
        in_specs=[pl.BlockSpec(memory_space=pltpu.SMEM),
                  pl.BlockSpec((1, tile, B_COLS), lambda b, c: (b, c, 0)),
                  pl.BlockSpec((tile, 2 * B_KW), lambda b, c: (c, 0)),
                  pl.BlockSpec((tile, 2 * B_KW), lambda b, c: (c, 0)),
                  pl.BlockSpec((2, B_W), const),
                  pl.BlockSpec((1, B_W, B_KW), lambda b, c: (b, 0, 0))],
        out_specs=[pl.BlockSpec((1, tile, B_W), lambda b, c: (b, c, 0)),
                   pl.BlockSpec((1, B_W, B_KW), lambda b, c: (b, 0, 0))],
        out_shape=[jax.ShapeDtypeStruct((bsz, t, B_W), F32),
                   jax.ShapeDtypeStruct((bsz, B_W, B_KW), F32)],
        scratch_shapes=[pltpu.VMEM((B_W, B_KW), F32), pltpu.VMEM((B_HEADS, tile, tile), F32),
                        pltpu.VMEM((tile, B_KW), F32), pltpu.VMEM((tile, B_W), F32)],
        compiler_params=_cparams("parallel", "arbitrary"),
        name="retention",
    )(log_gamma, pb, cos_t, sin_t, lg_rows, m0)


def _rwkv_body(pc_ref, sh0_ref, s0_ref, mu_ref, vec_ref, w2_ref, a2_ref, g2_ref,
               yc_ref, sout_ref, shout_ref, s_scr, sh_scr, *, nchunk):
    c = pl.program_id(1)
    L = RWKV_CHUNK
    tl = nchunk * L
    npair = C_HEADS // 2

    @pl.when(c == 0)
    def _():
        s_scr[...] = s0_ref[0]
        sh_scr[...] = sh0_ref[0]

    p = pc_ref[0]
    row = _iota2((tl, C_COLS), 0)
    prev = jnp.where(row == 0, sh_scr[...], pltpu.roll(p, 1, 0))
    xs = p + mu_ref[...] * (prev - p)
    sh_scr[...] = p[tl - 1:tl, :]

    r = xs[:, 0:C_W]
    k = xs[:, C_W:2 * C_W]
    v = xs[:, 2 * C_W:3 * C_W]
    lora = xs[:, 3 * C_W:C_COLS]
    w0, a0, k_k, k_a, r_k, ln_w, ln_b = (vec_ref[i:i + 1, :] for i in range(7))
    w = -_softplus(-(w0 + _mm(jnp.tanh(lora), w2_ref[...]))) - 0.5
    lw = -jnp.exp(w)
    a = _sigmoid(a0 + _mm(lora, a2_ref[...]))
    gate = _mm(_sigmoid(lora), g2_ref[...])
    kk = k * k_k
    kk = kk / jnp.maximum(jnp.sqrt(_seg_sum(kk * kk, C_DH)), 1e-6)
    kc = k * (1.0 + (a - 1.0) * k_a)
    beta = kk * a

    ti = _iota2((tl, tl), 0)
    si = _iota2((tl, tl), 1)
    same_chunk = (ti // L) == (si // L)
    b = _mm_exact_lhs((same_chunk & (ti >= si)).astype(BF16), lw)
    b_tot = jnp.broadcast_to(b.reshape(nchunk, L, C_W)[:, L - 1:L, :],
                             (nchunk, L, C_W)).reshape(tl, C_W)
    e_nb = jnp.exp(-b)
    e_tail = jnp.exp(b_tot - b)
    kap_t = kk * jnp.exp(b - lw)
    r_t = r * jnp.exp(b)
    beta_t = beta * e_nb
    k_t = kc * e_nb
    beta_h = beta * e_tail
    k_h = kc * e_tail
    p_last = jnp.exp(b_tot)

    gch = min(RWKV_GROUP_CHUNKS, nchunk)
    nblk = 2 * gch
    wcat = L * nblk
    t_cat = _iota2((L, wcat), 0)
    s_cat = _iota2((L, wcat), 1) % L
    blk_cat = _iota2((L, wcat), 1) // L
    strict_cat = t_cat > s_cat
    incl_cat = t_cat >= s_cat
    eye_cat = (t_cat == s_cat).astype(F32)
    head0 = (_iota2((L, LANES), 1) // C_DH) == 0
    hmask = [head0, jnp.logical_not(head0)]
    pair_diag = _block_ones(LANES, C_DH)
    zeros_p = jnp.zeros((L, LANES), BF16)

    cut = lambda x, ci, pr: x[ci * L:(ci + 1) * L, pr * LANES:(pr + 1) * LANES]
    items = [(ci, pr) for ci in range(nchunk) for pr in range(npair)]
    groups = [(pr, cg) for cg in range(nchunk // gch) for pr in range(npair)]
    members = lambda cg: range(cg * gch, (cg + 1) * gch)

    def block_diag(x):
        xb = x.astype(BF16)
        return jnp.concatenate([jnp.where(blk_cat == j, xb, jnp.zeros_like(xb)) for j in range(nblk)], axis=0)

    def placed(ci, parts):
        width = len(parts) * LANES
        body = parts[0] if len(parts) == 1 else jnp.concatenate(parts, axis=1)
        secs = [body if cj == ci % gch else jnp.zeros((L, width), BF16) for cj in range(gch)]
        return secs[0] if gch == 1 else jnp.concatenate(secs, axis=1)

    def head_part(x, hh):
        return jnp.where(hmask[hh], x.astype(BF16), zeros_p)

    a_cat, b_cat, c_cat, e_cat = {}, {}, {}, {}
    gram = {}
    for ci, pr in items:
        lhs = jnp.concatenate([jnp.where(hmask[hh], cut(x, ci, pr), 0.0)
                               for hh in range(2) for x in (kap_t, r_t)], axis=0)
        both = _mm_nt(lhs, jnp.concatenate([cut(beta_t, ci, pr), cut(k_t, ci, pr)], axis=0))
        gram[ci, pr, 0] = both[0:2 * L]
        gram[ci, pr, 1] = pltpu.roll(both[2 * L:4 * L], LANES // 2, 1)
    top = lambda g0, g1: jnp.where(head0, g0[0:L], g1[0:L])
    bot = lambda g0, g1: jnp.where(head0, g0[L:2 * L], g1[L:2 * L])
    join = lambda xs: xs[0] if gch == 1 else jnp.concatenate(xs, axis=1)
    for g in groups:
        pr, cg = g
        a_cat[g] = jnp.where(strict_cat, join([top(gram[ci, pr, 0], gram[ci, pr, 1]) for ci in members(cg)]), 0.0)
        b_cat[g] = jnp.where(strict_cat, join([top(gram[ci, pr, 1], gram[ci, pr, 0]) for ci in members(cg)]), 0.0)
        c_cat[g] = jnp.where(incl_cat, join([bot(gram[ci, pr, 0], gram[ci, pr, 1]) for ci in members(cg)]), 0.0)
        e_cat[g] = jnp.where(incl_cat, join([bot(gram[ci, pr, 1], gram[ci, pr, 0]) for ci in members(cg)]), 0.0)
    order_ac = (0, 1)
    order_be = (1, 0)

    n_pow = {g: -a_cat[g] for g in groups}
    t_inv = {g: eye_cat + n_pow[g] for g in groups}
    for _ in range(L.bit_length() - 2):
        n_pow = {g: jnp.dot(n_pow[g].astype(BF16), block_diag(n_pow[g]), preferred_element_type=F32)
                 for g in groups}
        upd = {g: jnp.dot(n_pow[g].astype(BF16), block_diag(t_inv[g]), preferred_element_type=F32)
               for g in groups}
        t_inv = {g: t_inv[g] + upd[g] for g in groups}

    def stacked(cg, order, part_fn):
        return jnp.concatenate([placed(ci, part_fn(ci, hh)) for ci in members(cg) for hh in order], axis=0)

    sec = lambda x, ci, width: x[:, (ci % gch) * width:(ci % gch + 1) * width]
    bv = {(pr, cg): jnp.dot(b_cat[pr, cg].astype(BF16),
                            stacked(cg, order_be, lambda ci, hh: [head_part(cut(v, ci, pr), hh)]),
                            preferred_element_type=F32) for pr, cg in groups}
    rows2 = lambda order, part_fn: jnp.concatenate(
        [jnp.concatenate(part_fn(hh), axis=1) for hh in order], axis=0)
    tr = {(ci, pr): jnp.dot(sec(t_inv[pr, ci // gch], ci, LANES).astype(BF16),
                            rows2(order_ac, lambda hh: [head_part(sec(bv[pr, ci // gch], ci, LANES), hh),
                                                        head_part(cut(kap_t, ci, pr), hh)]),
                            preferred_element_type=F32) for ci, pr in items}
    u0 = {it: -tr[it][:, 0:LANES] for it in items}
    wm = {it: tr[it][:, LANES:2 * LANES] for it in items}
    ce = {(ci, pr): jnp.dot(
        jnp.concatenate([sec(c_cat[pr, ci // gch], ci, LANES), sec(e_cat[pr, ci // gch], ci, LANES)],
                        axis=1).astype(BF16),
        jnp.concatenate([rows2(order_ac, lambda hh: [head_part(u0[ci, pr], hh), head_part(wm[ci, pr], hh)]),
                         rows2(order_be, lambda hh: [head_part(cut(v, ci, pr), hh), zeros_p])], axis=0),
        preferred_element_type=F32) for ci, pr in items}
    y0 = {it: ce[it][:, 0:LANES] for it in items}
    r_hat = {(ci, pr): cut(r_t, ci, pr) - ce[ci, pr][:, LANES:2 * LANES] for ci, pr in items}
    q_m = {it: -jnp.where(pair_diag, _mm_tn(wm[it], cut(beta_h, *it)), 0.0) for it in items}
    z_m = {it: jnp.where(pair_diag,
                         _mm_tn(jnp.concatenate([u0[it], cut(v, *it)], axis=0),
                                jnp.concatenate([cut(beta_h, *it), cut(k_h, *it)], axis=0)), 0.0)
           for it in items}

    s_mat = [s_scr[pr] for pr in range(npair)]
    y_rows = []
    for ci in range(nchunk):
        y_dep = [_mm_nt(r_hat[ci, pr], s_mat[pr]) for pr in range(npair)]
        s_q = [_mm(s_mat[pr], q_m[ci, pr]) for pr in range(npair)]
        s_mat = [s_mat[pr] * p_last[ci * L:ci * L + 1, pr * LANES:(pr + 1) * LANES] + s_q[pr] + z_m[ci, pr]
                 for pr in range(npair)]
        y_rows.append(jnp.concatenate([y_dep[pr] + y0[ci, pr] for pr in range(npair)], axis=-1))
    for pr in range(npair):
        s_scr[pr] = s_mat[pr]

    y = y_rows[0] if nchunk == 1 else jnp.concatenate(y_rows, axis=0)
    mu_y = _seg_sum(y, C_DH) * (1.0 / C_DH)
    yc = y - mu_y
    var = _seg_sum(yc * yc, C_DH) * (1.0 / C_DH)
    yn = yc * lax.rsqrt(var + RWKV_GN_EPS) * ln_w + ln_b
    bonus = _seg_sum(r * kc * r_k, C_DH) * v
    yc_ref[0] = (yn + bonus) * gate

    @pl.when(c == pl.num_programs(1) - 1)
    def _():
        sout_ref[0] = s_scr[...]
        shout_ref[0] = sh_scr[...]


def _rwkv(pc, sh0, s0, mu, vecs, w2p, a2p, g2p):
    bsz, t, _ = pc.shape
    L = min(RWKV_TILE, t)
    npair = C_HEADS // 2
    const = lambda b, c: (0, 0)
    return pl.pallas_call(
        functools.partial(_rwkv_body, nchunk=L // RWKV_CHUNK),
        grid=(bsz, t // L),
        in_specs=[pl.BlockSpec((1, L, C_COLS), lambda b, c: (b, c, 0)),
                  pl.BlockSpec((1, 1, C_COLS), lambda b, c: (b, 0, 0)),
                  pl.BlockSpec((1, npair, LANES, LANES), lambda b, c: (b, 0, 0, 0)),
                  pl.BlockSpec((1, C_COLS), const),
                  pl.BlockSpec((8, C_W), const),
                  pl.BlockSpec((C_LORA, C_W), const),
                  pl.BlockSpec((C_LORA, C_W), const),
                  pl.BlockSpec((C_LORA, C_W), const)],
        out_specs=[pl.BlockSpec((1, L, C_W), lambda b, c: (b, c, 0)),
                   pl.BlockSpec((1, npair, LANES, LANES), lambda b, c: (b, 0, 0, 0)),
                   pl.BlockSpec((1, 1, C_COLS), lambda b, c: (b, 0, 0))],
        out_shape=[jax.ShapeDtypeStruct((bsz, t, C_W), F32),
                   jax.ShapeDtypeStruct((bsz, npair, LANES, LANES), F32),
                   jax.ShapeDtypeStruct((bsz, 1, C_COLS), F32)],
        scratch_shapes=[pltpu.VMEM((npair, LANES, LANES), F32), pltpu.VMEM((1, C_COLS), F32)],
        compiler_params=_cparams("parallel", "arbitrary"),
        name="rwkv",
    )(pc, sh0, s0, mu, vecs, w2p, a2p, g2p)


def _outattn_body(x_ref, oa_ref, ob_ref, yc_ref, woa_ref, wob_ref, woc_ref, g_ref, wq_ref, wo_ref,
                  mk_ref, mv_ref, o_ref):
    tm, d = x_ref.shape[1], x_ref.shape[2]
    hd = d // MEM_HEADS
    subs = _row_subtiles(tm)
    heads = [slice(h * hd, (h + 1) * hd) for h in range(MEM_HEADS)]
    mk = mk_ref[0].astype(BF16)
    mv = mv_ref[0].astype(BF16)
    x = [x_ref[0, r, :] + _mm(oa_ref[0, r, :], woa_ref[...]) + _mm(ob_ref[0, r, :], wob_ref[...])
         + _mm(yc_ref[0, r, :], woc_ref[...]) for r in subs]
    q = [_mm(_rms(xi, g_ref[...]), wq_ref[...]) for xi in x]
    s = [[_mm_nt(qi[:, c], mk[:, c]) * (hd ** -0.5) for c in heads] for qi in q]
    o = []
    for si in s:
        e = [jnp.exp(sh - jnp.max(sh, axis=-1, keepdims=True)) for sh in si]
        pr = [eh * (1.0 / jnp.sum(eh, axis=-1, keepdims=True)) for eh in e]
        o.append([_mm(ph, mv[:, c]) for ph, c in zip(pr, heads)])
    for r, xi, oi in zip(subs, x, o):
        o_ref[0, r, :] = xi + _mm(jnp.concatenate(oi, axis=-1), wo_ref[...])


def _outattn(x, oa, ob, yc, woa, wob, woc, g, wq, wo, mk, mv):
    bsz, t, d = x.shape
    tm = min(TOK_TILE, t)
    n_mem = mk.shape[1]
    const = lambda b, i: (0, 0)
    tok = lambda w: pl.BlockSpec((1, tm, w), lambda b, i: (b, i, 0))
    full = lambda a: pl.BlockSpec(a.shape, const)
    mem = pl.BlockSpec((1, n_mem, d), lambda b, i: (b, 0, 0))
    return pl.pallas_call(
        _outattn_body,
        grid=(bsz, t // tm),
        in_specs=[tok(d), tok(A_W), tok(B_W), tok(C_W), full(woa), full(wob), full(woc), full(g),
                  full(wq), full(wo), mem, mem],
        out_specs=tok(d),
        out_shape=jax.ShapeDtypeStruct((bsz, t, d), F32),
        compiler_params=_cparams("parallel", "parallel"),
        name="outattn",
    )(x, oa, ob, yc, woa, wob, woc, g, wq, wo, mk, mv)


def _blockdiag_t(s):
    bsz, h, dk, dv = s.shape
    eye = jnp.eye(h, dtype=s.dtype)
    return jnp.einsum('bhkv,hg->bhvgk', s, eye).reshape(bsz, h * dv, h * dk)


def _unblockdiag_t(m, h, dk, dv):
    bsz = m.shape[0]
    m5 = m.reshape(bsz, h, dv, h, dk)
    return jnp.stack([m5[:, i, :, i, :] for i in range(h)], axis=1).transpose(0, 1, 3, 2)


def _pair_pack(s):
    bsz, h, d, _ = s.shape
    s4 = s.reshape(bsz, h // 2, 2, d, d)
    eye = jnp.eye(2, dtype=s.dtype)
    return jnp.einsum('bphvk,hg->bphvgk', s4, eye).reshape(bsz, h // 2, 2 * d, 2 * d)


def _pair_unpack(m):
    bsz, npair, w, _ = m.shape
    d = w // 2
    m6 = m.reshape(bsz, npair, 2, d, 2, d)
    return jnp.stack([m6[:, :, i, :, i, :] for i in range(2)], axis=2).reshape(bsz, 2 * npair, d, d)


def _rope_tables(pos):
    half = B_DK // 2
    inv = ROPE_BASE ** (-jnp.arange(half, dtype=F32) / half)
    ang = pos.astype(F32)[:, None] * inv[None, :]
    cos = jnp.tile(jnp.cos(ang), (1, 2 * 2 * B_HEADS))
    sin = jnp.tile(jnp.sin(ang), (1, 2 * 2 * B_HEADS))
    return cos, sin


def _prep_layer(l, W, lbs, log_gamma):
    row = lambda a: a.reshape(1, -1).astype(F32)
    w_in = W['w_in'][l]
    pad_rows = lambda m, off: jnp.zeros((C_LORA, C_W), F32).at[off:off + m.shape[0]].set(m).astype(BF16)
    lb = lbs[l]
    zero = jnp.zeros((C_W,), F32)
    vecs = jnp.stack([W['rwkv_w0'][l], W['rwkv_a0'][l], W['rwkv_k_k'][l], W['rwkv_k_a'][l],
                      W['rwkv_r_k'][l].reshape(-1), W['rwkv_ln_w'][l], W['rwkv_ln_b'][l], zero])
    lg_rows = jnp.stack([jnp.pad(jnp.repeat(log_gamma, B_DK), (0, B_W - B_KW)),
                         jnp.repeat(log_gamma, B_DV)])
    w_out = W['w_out'][l].astype(BF16)
    return dict(
        n1=row(W['norm_ffn1'][l]), g1=W['ffn1_wg'][l].astype(BF16), u1=W['ffn1_wu'][l].astype(BF16),
        d1=W['ffn1_wd'][l].astype(BF16),
        n2=row(W['norm_ffn2'][l]), g2=W['ffn2_wg'][l].astype(BF16), u2=W['ffn2_wu'][l].astype(BF16),
        d2=W['ffn2_wd'][l].astype(BF16),
        nmix=row(W['norm_mix'][l]),
        wa=w_in[:, :A_COLS].astype(BF16), wb=w_in[:, A_COLS:A_COLS + B_COLS].astype(BF16),
        wc=w_in[:, A_COLS + B_COLS:].astype(BF16),
        lb_floor=row(jnp.maximum(lb, LB_FLOOR)), one_minus_lb=row(1.0 - lb),
        gn=row(jnp.tile(W['hgrn_norm'][l], A_HEADS)),
        lg_rows=lg_rows, log_gamma=log_gamma,
        mu=row(W['rwkv_mu'][l]), vecs=vecs,
        w2p=pad_rows(W['rwkv_w2'][l], 0), a2p=pad_rows(W['rwkv_a2'][l], C_DECAY_RANK),
        g2p=pad_rows(W['rwkv_g2'][l], C_DECAY_RANK + C_AAA_RANK),
        woa=w_out[:A_W], wob=w_out[A_W:A_W + B_W], woc=w_out[A_W + B_W:],
        ncross=row(W['norm_cross'][l]), wq=W['wq_x'][l].astype(BF16), wo=W['wo_x'][l].astype(BF16),
    )


def _trunk(x, pos, mem_k, mem_v, st_a, st_b, st_c, st_sh, layers, final_g):
    bsz, t, d = x.shape
    n = bsz * t
    depth = len(layers)
    cos_t, sin_t = _rope_tables(pos)
    new_a, new_b, new_c, new_sh = [], [], [], []
    x2 = x.reshape(n, d)
    for l, p in enumerate(layers):
        x2 = _ffn(x2, p['n1'], p['g1'], p['u1'], p['d1'], final_g, False)
        pa, pb, pc = _inproj(x2, p['nmix'], p['wa'], p['wb'], p['wc'])
        oa, ma = _hgrn(pa.reshape(bsz, t, A_COLS), p['lb_floor'], p['one_minus_lb'], p['gn'],
                       _pair_pack(st_a[l].transpose(0, 1, 3, 2)))
        ob, mb = _retention(p['log_gamma'], pb.reshape(bsz, t, B_COLS), cos_t, sin_t, p['lg_rows'],
                            _blockdiag_t(st_b[l]))
        yc, sc, sh = _rwkv(pc.reshape(bsz, t, C_COLS), st_sh[l], _pair_pack(st_c[l]), p['mu'],
                           p['vecs'], p['w2p'], p['a2p'], p['g2p'])
        x3 = _outattn(x2.reshape(bsz, t, d), oa, ob, yc, p['woa'], p['wob'], p['woc'], p['ncross'],
                      p['wq'], p['wo'], mem_k[l].reshape(bsz, -1, d), mem_v[l].reshape(bsz, -1, d))
        x2 = _ffn(x3.reshape(n, d), p['n2'], p['g2'], p['u2'], p['d2'], final_g, l == depth - 1)
        new_a.append(_pair_unpack(ma).transpose(0, 1, 3, 2))
        new_b.append(_unblockdiag_t(mb, B_HEADS, B_DK, B_DV))
        new_c.append(_pair_unpack(sc))
        new_sh.append(sh)
    return (x2.reshape(bsz, t, d), jnp.stack(new_a), jnp.stack(new_b), jnp.stack(new_c),
            jnp.stack(new_sh))


def kernel(x_prompt, x_sample, mem_prompt, cache_mem_k, cache_mem_v, state_hgrn, state_ret, state_rwkv, state_rwkv_shift, norm_ffn1, ffn1_wg, ffn1_wu, ffn1_wd, norm_mix, w_in, hgrn_lb_param, hgrn_norm, rwkv_mu, rwkv_w0, rwkv_w2, rwkv_a0, rwkv_a2, rwkv_g2, rwkv_k_k, rwkv_k_a, rwkv_r_k, rwkv_ln_w, rwkv_ln_b, w_out, norm_cross, norm_mem, wq_x, wk_x, wv_x, wo_x, norm_ffn2, ffn2_wg, ffn2_wu, ffn2_wd, final_norm):
    W = dict(norm_ffn1=norm_ffn1, ffn1_wg=ffn1_wg, ffn1_wu=ffn1_wu, ffn1_wd=ffn1_wd, norm_mix=norm_mix,
             w_in=w_in, hgrn_norm=hgrn_norm, rwkv_mu=rwkv_mu, rwkv_w0=rwkv_w0, rwkv_w2=rwkv_w2,
             rwkv_a0=rwkv_a0, rwkv_a2=rwkv_a2, rwkv_g2=rwkv_g2, rwkv_k_k=rwkv_k_k, rwkv_k_a=rwkv_k_a,
             rwkv_r_k=rwkv_r_k, rwkv_ln_w=rwkv_ln_w, rwkv_ln_b=rwkv_ln_b, w_out=w_out,
             norm_cross=norm_cross, wq_x=wq_x, wo_x=wo_x, norm_ffn2=norm_ffn2, ffn2_wg=ffn2_wg,
             ffn2_wu=ffn2_wu, ffn2_wd=ffn2_wd)
    depth = w_in.shape[0]
    d = x_prompt.shape[-1]
    sm = jax.nn.softmax(hgrn_lb_param.astype(F32), axis=0)
    lbs = jnp.cumsum(sm, axis=0) - sm[0:1]
    log_gamma = jnp.log1p(-jnp.exp2(-5.0 - jnp.arange(B_HEADS, dtype=F32)))
    layers = [_prep_layer(l, W, lbs, log_gamma) for l in range(depth)]
    final_g = final_norm.reshape(1, -1).astype(F32)

    bp, tp, _ = x_prompt.shape
    n_mem = mem_prompt.shape[1]
    hd = d // MEM_HEADS
    mem2 = mem_prompt.reshape(bp * n_mem, d)
    mks, mvs = [], []
    for l in range(depth):
        mk, mv = _memkv(mem2, norm_mem[l].reshape(1, -1), wk_x[l].astype(BF16), wv_x[l].astype(BF16))
        mks.append(mk.reshape(bp, n_mem, MEM_HEADS, hd))
        mvs.append(mv.reshape(bp, n_mem, MEM_HEADS, hd))
    p_mem_k = jnp.stack(mks)
    p_mem_v = jnp.stack(mvs)

    zeros = lambda *s: jnp.zeros((depth, bp) + s, F32)
    y_prompt, p_hgrn, p_ret, p_rwkv, p_shift = _trunk(
        x_prompt, jnp.arange(tp, dtype=jnp.int32), p_mem_k, p_mem_v,
        zeros(A_HEADS, A_DK, A_DV), zeros(B_HEADS, B_DK, B_DV), zeros(C_HEADS, C_DH, C_DH),
        zeros(1, C_COLS), layers, final_g)

    ts = x_sample.shape[1]
    y_sample, s_hgrn, s_ret, s_rwkv, s_shift = _trunk(
        x_sample, PAST_LEN + jnp.arange(ts, dtype=jnp.int32), cache_mem_k, cache_mem_v,
        state_hgrn, state_ret, state_rwkv, state_rwkv_shift, layers, final_g)

    return (y_prompt, y_sample, p_hgrn, p_ret, p_rwkv, p_shift, p_mem_k, p_mem_v,
            s_hgrn, s_ret, s_rwkv, s_shift)
```

```python
import functools

import jax
import jax.numpy as jnp
from jax import lax
from jax.experimental import pallas as pl
from jax.experimental.pallas import tpu as pltpu

F32 = jnp.float32
BF16 = jnp.bfloat16

EPS = 1e-6
LB_FLOOR = 1e-20
ROPE_BASE = 10000.0
PAST_LEN = 4096
MEM_HEADS = 4
A_HEADS, A_DK, A_DV = 4, 64, 64
A_KW = A_HEADS * A_DK
A_W = A_HEADS * A_DV
A_COLS = 2 * A_KW + 2 * A_W
B_HEADS, B_DK, B_DV = 6, 32, 64
B_KW = B_HEADS * B_DK
B_W = B_HEADS * B_DV
B_COLS = 2 * B_KW + 2 * B_W
C_HEADS, C_DH = 6, 64
C_W = C_HEADS * C_DH
C_DECAY_RANK, C_AAA_RANK, C_GATE_RANK = 32, 32, 64
C_LORA = C_DECAY_RANK + C_AAA_RANK + C_GATE_RANK
C_COLS = 3 * C_W + C_LORA
RWKV_GN_EPS = 64e-5
LOG2E = 1.4426950408889634

LANES = 128
VMEM_LIMIT = 56 * 1024 * 1024
FFN_TILE = 1024
FFN_FCHUNK = 256
TOK_TILE = 512
INPROJ_TILE = 1024
ATTN_TILE = 1024
SUB_ROWS = 256
HGRN_BLOCK = 16
HGRN_TILE = 256
RET_TILE = 256
RWKV_CHUNK = 64
RWKV_TILE = 256
RWKV_GROUP_CHUNKS = 2


def _cparams(*sem):
    return pltpu.CompilerParams(dimension_semantics=sem, vmem_limit_bytes=VMEM_LIMIT)


def _mm(a, b):
    return jnp.dot(a.astype(BF16), b.astype(BF16), preferred_element_type=F32)


def _mm_nt(a, b):
    return lax.dot_general(a.astype(BF16), b.astype(BF16), (((1,), (1,)), ((), ())),
                           preferred_element_type=F32)


def _mm_tn(a, b):
    return lax.dot_general(a.astype(BF16), b.astype(BF16), (((0,), (0,)), ((), ())),
                           preferred_element_type=F32)


def _split3(x):
    hi = x.astype(BF16)
    r = x - hi.astype(F32)
    mid = r.astype(BF16)
    lo = (r - mid.astype(F32)).astype(BF16)
    return hi, mid, lo


def _mm_exact_lhs(c, x):
    hi, mid, lo = _split3(x)
    return (jnp.dot(c, hi, preferred_element_type=F32) + jnp.dot(c, mid, preferred_element_type=F32)
            + jnp.dot(c, lo, preferred_element_type=F32))


def _rms(x, g):
    return x * lax.rsqrt(jnp.mean(x * x, axis=-1, keepdims=True) + EPS) * g


def _sigmoid(x):
    return 0.5 * jnp.tanh(0.5 * x) + 0.5


def _silu(x):
    return x * _sigmoid(x)


def _softplus(x):
    return jnp.maximum(x, 0.0) + jnp.log(1.0 + jnp.exp(-jnp.abs(x)))


def _row_subtiles(rows):
    n = max(1, rows // SUB_ROWS)
    step = rows // n
    return [slice(i * step, (i + 1) * step) for i in range(n)]


def _iota2(shape, axis):
    return lax.broadcasted_iota(jnp.int32, shape, axis)


def _block_ones(n, blk):
    return (_iota2((n, n), 0) // blk) == (_iota2((n, n), 1) // blk)


def _seg_sum(x, blk):
    n = x.shape[-1]
    outs = []
    for lo in range(0, n, LANES):
        w = min(LANES, n - lo)
        outs.append(jnp.dot(x[:, lo:lo + w].astype(BF16), _block_ones(w, blk).astype(BF16),
                            preferred_element_type=F32))
    return outs[0] if len(outs) == 1 else jnp.concatenate(outs, axis=-1)


def _ffn_body(x_ref, g_ref, wg_ref, wu_ref, wd_ref, gf_ref, o_ref, *, apply_final_norm):
    x = x_ref[...]
    hb = _rms(x, g_ref[...]).astype(BF16)
    d_ff = wg_ref.shape[1]
    acc = jnp.zeros(x.shape, F32)
    for j in range(d_ff // FFN_FCHUNK):
        cols = slice(j * FFN_FCHUNK, (j + 1) * FFN_FCHUNK)
        a = jnp.dot(hb, wg_ref[:, cols], preferred_element_type=F32)
        u = jnp.dot(hb, wu_ref[:, cols], preferred_element_type=F32)
        t = (_silu(a) * u).astype(BF16)
        acc = acc + jnp.dot(t, wd_ref[cols, :], preferred_element_type=F32)
    y = x + 0.5 * acc
    if apply_final_norm:
        y = _rms(y, gf_ref[...])
    o_ref[...] = y


def _ffn(x, g, wg, wu, wd, gf, apply_final_norm):
    n, d = x.shape
    d_ff = wg.shape[1]
    tm = min(FFN_TILE, n)
    const = lambda i: (0, 0)
    resident = lambda shape: pl.BlockSpec(shape, const, pipeline_mode=pl.Buffered(1))
    return pl.pallas_call(
        functools.partial(_ffn_body, apply_final_norm=apply_final_norm),
        grid=(n // tm,),
        in_specs=[pl.BlockSpec((tm, d), lambda i: (i, 0)),
                  pl.BlockSpec((1, d), const),
                  resident((d, d_ff)),
                  resident((d, d_ff)),
                  resident((d_ff, d)),
                  pl.BlockSpec((1, d), const)],
        out_specs=pl.BlockSpec((tm, d), lambda i: (i, 0)),
        out_shape=jax.ShapeDtypeStruct((n, d), F32),
        compiler_params=_cparams("parallel"),
        name="ffn",
    )(x, g, wg, wu, wd, gf)


def _inproj_body(x_ref, g_ref, wa_ref, wb_ref, wc_ref, pa_ref, pb_ref, pc_ref):
    hb = _rms(x_ref[...], g_ref[...]).astype(BF16)
    pa_ref[...] = jnp.dot(hb, wa_ref[...], preferred_element_type=F32)
    pb_ref[...] = jnp.dot(hb, wb_ref[...], preferred_element_type=F32)
    pc_ref[...] = jnp.dot(hb, wc_ref[...], preferred_element_type=F32)


def _inproj(x, g, wa, wb, wc):
    n, d = x.shape
    tm = min(INPROJ_TILE, n)
    const = lambda i: (0, 0)
    row = lambda i: (i, 0)
    widths = (wa.shape[1], wb.shape[1], wc.shape[1])
    return pl.pallas_call(
        _inproj_body,
        grid=(n // tm,),
        in_specs=[pl.BlockSpec((tm, d), row), pl.BlockSpec((1, d), const)]
                 + [pl.BlockSpec((d, w), const, pipeline_mode=pl.Buffered(1)) for w in widths],
        out_specs=[pl.BlockSpec((tm, w), row) for w in widths],
        out_shape=[jax.ShapeDtypeStruct((n, w), F32) for w in widths],
        compiler_params=_cparams("parallel"),
        name="inproj",
    )(x, g, wa, wb, wc)


def _memkv_body(m_ref, g_ref, wk_ref, wv_ref, k_ref, v_ref):
    hb = _rms(m_ref[...], g_ref[...]).astype(BF16)
    k_ref[...] = jnp.dot(hb, wk_ref[...], preferred_element_type=F32)
    v_ref[...] = jnp.dot(hb, wv_ref[...], preferred_element_type=F32)


def _memkv(mem, g, wk, wv):
    n, d = mem.shape
    tm = min(TOK_TILE, n)
    const = lambda i: (0, 0)
    row = lambda i: (i, 0)
    return pl.pallas_call(
        _memkv_body,
        grid=(n // tm,),
        in_specs=[pl.BlockSpec((tm, d), row), pl.BlockSpec((1, d), const),
                  pl.BlockSpec((d, d), const), pl.BlockSpec((d, d), const)],
        out_specs=[pl.BlockSpec((tm, d), row), pl.BlockSpec((tm, d), row)],
        out_shape=[jax.ShapeDtypeStruct((n, d), F32)] * 2,
        compiler_params=_cparams("parallel"),
        name="memkv",
    )(mem, g, wk, wv)


def _hgrn_body(pa_ref, lbf_ref, oml_ref, gn_ref, m0_ref, oa_ref, mout_ref, m_scr, *, tile):
    c = pl.program_id(1)
    blk = HGRN_BLOCK
    nb = tile // blk

    @pl.when(c == 0)
    def _():
        m_scr[...] = m0_ref[0]

    x = pa_ref[0]
    q = x[:, 0:A_KW]
    f = x[:, A_KW:2 * A_KW]
    v = x[:, 2 * A_KW:2 * A_KW + A_W]
    g = x[:, 2 * A_KW + A_W:A_COLS]
    gate_f = lbf_ref[...] + oml_ref[...] * _sigmoid(f)
    logf = jnp.log(gate_f)
    ka = 1.0 - gate_f
    qa = _silu(q) * (A_DK ** -0.5)
    ti = _iota2((tile, tile), 0)
    si = _iota2((tile, tile), 1)
    same_blk = (ti // blk) == (si // blk)
    b = _mm_exact_lhs((same_blk & (ti >= si)).astype(BF16), logf)
    b_tot = jnp.broadcast_to(b.reshape(nb, blk, A_KW)[:, blk - 1:blk, :],
                             (nb, blk, A_KW)).reshape(tile, A_KW)
    q_t = qa * jnp.exp(b)
    k_hat = ka * jnp.exp(b_tot - b)
    d_blk = jnp.exp(b_tot)
    head_ones = _block_ones(A_KW, A_DK)
    head_ones_b = head_ones.astype(BF16)

    hb = blk // 2
    halves = lambda a: (lambda a4: (a4[:, 0], a4[:, 1]))(a.reshape(nb, 2, hb, A_KW))
    b_h, ka_h, qa_h, v_h = halves(b * LOG2E), halves(ka), halves(qa), halves(v)
    src_row = lambda a_h, s: a_h[s // hb][:, s % hb:s % hb + 1, :]
    row_h = lax.broadcasted_iota(jnp.int32, (nb, hb, A_KW), 1)
    slab_keys, slabs = [], []
    for s in range(blk):
        for half in range(s // hb, 2):
            z = qa_h[half] * src_row(ka_h, s) * jnp.exp2(jnp.minimum(b_h[half] - src_row(b_h, s), 0.0))
            if half == s // hb:
                z = jnp.where(row_h >= s % hb, z, 0.0)
            slab_keys.append((s, half))
            slabs.append(z.reshape(nb * hb, A_KW))
    att = _mm(jnp.concatenate(slabs, axis=0), head_ones_b)
    o_h = [jnp.zeros((nb, hb, A_W), F32), jnp.zeros((nb, hb, A_W), F32)]
    for i, (s, half) in enumerate(slab_keys):
        part = att[i * nb * hb:(i + 1) * nb * hb, :].reshape(nb, hb, A_W)
        o_h[half] = o_h[half] + part * src_row(v_h, s)
    o3 = jnp.stack(o_h, axis=1)

    npair = A_HEADS // 2
    pair_diag = _block_ones(LANES, A_DK)
    part = lambda a, j, p: a[j * blk:(j + 1) * blk, p * LANES:(p + 1) * LANES]
    outer = [[jnp.where(pair_diag, _mm_tn(part(v, j, p), part(k_hat, j, p)), 0.0) for p in range(npair)]
             for j in range(nb)]
    m = [m_scr[p] for p in range(npair)]
    states = []
    for j in range(nb):
        states.append(m)
        m = [m[p] * d_blk[j * blk:j * blk + 1, p * LANES:(p + 1) * LANES] + outer[j][p] for p in range(npair)]
    for p in range(npair):
        m_scr[p] = m[p]
    o = jnp.concatenate(
        [jnp.concatenate([_mm_nt(part(q_t, j, p), states[j][p]) for p in range(npair)], axis=1)
         for j in range(nb)], axis=0)
    o = o + o3.reshape(tile, A_W)
    ms = _mm(o * o, head_ones_b) * (1.0 / A_DV)
    oa_ref[0] = o * lax.rsqrt(ms + EPS) * gn_ref[...] * _silu(g)

    @pl.when(c == pl.num_programs(1) - 1)
    def _():
        mout_ref[0] = m_scr[...]


def _hgrn(pa, lb_floor, one_minus_lb, gn, m0):
    bsz, t, _ = pa.shape
    tile = min(HGRN_TILE, t)
    npair = A_HEADS // 2
    const = lambda b, c: (0, 0)
    return pl.pallas_call(
        functools.partial(_hgrn_body, tile=tile),
        grid=(bsz, t // tile),
        in_specs=[pl.BlockSpec((1, tile, A_COLS), lambda b, c: (b, c, 0)),
                  pl.BlockSpec((1, A_KW), const), pl.BlockSpec((1, A_KW), const),
                  pl.BlockSpec((1, A_W), const),
                  pl.BlockSpec((1, npair, LANES, LANES), lambda b, c: (b, 0, 0, 0))],
        out_specs=[pl.BlockSpec((1, tile, A_W), lambda b, c: (b, c, 0)),
                   pl.BlockSpec((1, npair, LANES, LANES), lambda b, c: (b, 0, 0, 0))],
        out_shape=[jax.ShapeDtypeStruct((bsz, t, A_W), F32),
                   jax.ShapeDtypeStruct((bsz, npair, LANES, LANES), F32)],
        scratch_shapes=[pltpu.VMEM((npair, LANES, LANES), F32)],
        compiler_params=_cparams("parallel", "arbitrary"),
        name="hgrn",
    )(pa, lb_floor, one_minus_lb, gn, m0)


def _ret_body(lgs_ref, pb_ref, cos_ref, sin_ref, lg_ref, m0_ref, ob_ref, mout_ref, m_scr, d_scr, eq_scr, ev_scr,
              *, tile):
    c = pl.program_id(1)

    @pl.when(c == 0)
    def _():
        m_scr[...] = m0_ref[0]
        rel = (_iota2((tile, tile), 0) - _iota2((tile, tile), 1)).astype(F32)
        for h in range(B_HEADS):
            d_scr[h] = jnp.where(rel >= 0.0, jnp.exp(lgs_ref[h] * jnp.maximum(rel, 0.0)), 0.0)
        j_k = _iota2((tile, B_KW), 0).astype(F32)
        j_v = _iota2((tile, B_W), 0).astype(F32)
        eq_scr[...] = jnp.exp(lg_ref[0:1, 0:B_KW] * (j_k + 1.0))
        ev_scr[...] = jnp.exp(lg_ref[1:2, :] * (tile - 1.0 - j_v))

    x = pb_ref[0]
    qk = x[:, 0:2 * B_KW]
    v = x[:, 2 * B_KW:2 * B_KW + B_W]
    g = x[:, 2 * B_KW + B_W:B_COLS]
    half = B_DK // 2
    first_half = (_iota2(qk.shape, 1) % B_DK) < half
    swapped = jnp.where(first_half, -pltpu.roll(qk, 2 * B_KW - half, 1), pltpu.roll(qk, half, 1))
    qk = qk * cos_ref[...] + swapped * sin_ref[...]
    q = qk[:, 0:B_KW]
    k = qk[:, B_KW:2 * B_KW] * (B_DK ** -0.5)

    lg_k = lg_ref[0:1, 0:B_KW]
    m = m_scr[...]
    cross = _mm_nt(q * eq_scr[...], m)
    scores = [_mm_nt(q[:, h * B_DK:(h + 1) * B_DK], k[:, h * B_DK:(h + 1) * B_DK]) for h in range(B_HEADS)]
    outs = [_mm(scores[h] * d_scr[h], v[:, h * B_DV:(h + 1) * B_DV]) for h in range(B_HEADS)]
    o = jnp.concatenate(outs, axis=-1) + cross
    mu = _seg_sum(o, B_DV) * (1.0 / B_DV)
    oc = o - mu
    var = _seg_sum(oc * oc, B_DV) * (1.0 / B_DV)
    ob_ref[0] = oc * lax.rsqrt(var + EPS) * _silu(g)
    v_w = v * ev_scr[...]
    head_mask = (_iota2((B_W, B_KW), 0) // B_DV) == (_iota2((B_W, B_KW), 1) // B_DK)
    m_scr[...] = m * jnp.exp(lg_k * float(tile)) + jnp.where(head_mask, _mm_tn(v_w, k), 0.0)

    @pl.when(c == pl.num_programs(1) - 1)
    def _():
        mout_ref[0] = m_scr[...]


def _retention(log_gamma, pb, cos_t, sin_t, lg_rows, m0):
    bsz, t, _ = pb.shape
    tile = min(RET_TILE, t)
    const = lambda b, c: (0, 0)
    return pl.pallas_call(
        functools.partial(_ret_body, tile=tile),
        grid=(bsz, t // tile),
        in_specs=[pl.BlockSpec(memory_space=pltpu.SMEM),
                  pl.BlockSpec((1, tile, B_COLS), lambda b, c: (b, c, 0)),
                  pl.BlockSpec((tile, 2 * B_KW), lambda b, c: (c, 0)),
                  pl.BlockSpec((tile, 2 * B_KW), lambda b, c: (c, 0)),
                  pl.BlockSpec((2, B_W), const),
                  pl.BlockSpec((1, B_W, B_KW), lambda b, c: (b, 0, 0))],
        out_specs=[pl.BlockSpec((1, tile, B_W), lambda b, c: (b, c, 0)),
                   pl.BlockSpec((1, B_W, B_KW), lambda b, c: (b, 0, 0))],
        out_shape=[jax.ShapeDtypeStruct((bsz, t, B_W), F32),
                   jax.ShapeDtypeStruct((bsz, B_W, B_KW), F32)],
        scratch_shapes=[pltpu.VMEM((B_W, B_KW), F32), pltpu.VMEM((B_HEADS, tile, tile), F32),
                        pltpu.VMEM((tile, B_KW), F32), pltpu.VMEM((tile, B_W), F32)],
        compiler_params=_cparams("parallel", "arbitrary"),
        name="retention",
    )(log_gamma, pb, cos_t, sin_t, lg_rows, m0)


def _rwkv_body(pc_ref, sh0_ref, s0_ref, mu_ref, vec_ref, w2_ref, a2_ref, g2_ref,
               yc_ref, sout_ref, shout_ref, s_scr, sh_scr, *, nchunk):
    c = pl.program_id(1)
    L = RWKV_CHUNK
    tl = nchunk * L
    npair = C_HEADS // 2

    @pl.when(c == 0)
    def _():
        s_scr[...] = s0_ref[0]
        sh_scr[...] = sh0_ref[0]

    p = pc_ref[0]
    row = _iota2((tl, C_COLS), 0)
    prev = jnp.where(row == 0, sh_scr[...], pltpu.roll(p, 1, 0))
    xs = p + mu_ref[...] * (prev - p)
    sh_scr[...] = p[tl - 1:tl, :]

    r = xs[:, 0:C_W]
    k = xs[:, C_W:2 * C_W]
    v = xs[:, 2 * C_W:3 * C_W]
    lora = xs[:, 3 * C_W:C_COLS]
    w0, a0, k_k, k_a, r_k, ln_w, ln_b = (vec_ref[i:i + 1, :] for i in range(7))
    w = -_softplus(-(w0 + _mm(jnp.tanh(lora), w2_ref[...]))) - 0.5
    lw = -jnp.exp(w)
    a = _sigmoid(a0 + _mm(lora, a2_ref[...]))
    gate = _mm(_sigmoid(lora), g2_ref[...])
    kk = k * k_k
    kk = kk / jnp.maximum(jnp.sqrt(_seg_sum(kk * kk, C_DH)), 1e-6)
    kc = k * (1.0 + (a - 1.0) * k_a)
    beta = kk * a

    ti = _iota2((tl, tl), 0)
    si = _iota2((tl, tl), 1)
    same_chunk = (ti // L) == (si // L)
    b = _mm_exact_lhs((same_chunk & (ti >= si)).astype(BF16), lw)
    b_tot = jnp.broadcast_to(b.reshape(nchunk, L, C_W)[:, L - 1:L, :],
                             (nchunk, L, C_W)).reshape(tl, C_W)
    e_nb = jnp.exp(-b)
    e_tail = jnp.exp(b_tot - b)
    kap_t = kk * jnp.exp(b - lw)
    r_t = r * jnp.exp(b)
    beta_t = beta * e_nb
    k_t = kc * e_nb
    beta_h = beta * e_tail
    k_h = kc * e_tail
    p_last = jnp.exp(b_tot)

    gch = min(RWKV_GROUP_CHUNKS, nchunk)
    nblk = 2 * gch
    wcat = L * nblk
    t_cat = _iota2((L, wcat), 0)
    s_cat = _iota2((L, wcat), 1) % L
    blk_cat = _iota2((L, wcat), 1) // L
    strict_cat = t_cat > s_cat
    incl_cat = t_cat >= s_cat
    eye_cat = (t_cat == s_cat).astype(F32)
    head0 = (_iota2((L, LANES), 1) // C_DH) == 0
    hmask = [head0, jnp.logical_not(head0)]
    pair_diag = _block_ones(LANES, C_DH)
    zeros_p = jnp.zeros((L, LANES), BF16)

    cut = lambda x, ci, pr: x[ci * L:(ci + 1) * L, pr * LANES:(pr + 1) * LANES]
    items = [(ci, pr) for ci in range(nchunk) for pr in range(npair)]
    groups = [(pr, cg) for cg in range(nchunk // gch) for pr in range(npair)]
    members = lambda cg: range(cg * gch, (cg + 1) * gch)

    def block_diag(x):
        xb = x.astype(BF16)
        return jnp.concatenate([jnp.where(blk_cat == j, xb, jnp.zeros_like(xb)) for j in range(nblk)], axis=0)

    def placed(ci, parts):
        width = len(parts) * LANES
        body = parts[0] if len(parts) == 1 else jnp.concatenate(parts, axis=1)
        secs = [body if cj == ci % gch else jnp.zeros((L, width), BF16) for cj in range(gch)]
        return secs[0] if gch == 1 else jnp.concatenate(secs, axis=1)

    def head_part(x, hh):
        return jnp.where(hmask[hh], x.astype(BF16), zeros_p)

    a_cat, b_cat, c_cat, e_cat = {}, {}, {}, {}
    gram = {}
    for ci, pr in items:
        lhs = jnp.concatenate([jnp.where(hmask[hh], cut(x, ci, pr), 0.0)
                               for hh in range(2) for x in (kap_t, r_t)], axis=0)
        both = _mm_nt(lhs, jnp.concatenate([cut(beta_t, ci, pr), cut(k_t, ci, pr)], axis=0))
        gram[ci, pr, 0] = both[0:2 * L]
        gram[ci, pr, 1] = pltpu.roll(both[2 * L:4 * L], LANES // 2, 1)
    top = lambda g0, g1: jnp.where(head0, g0[0:L], g1[0:L])
    bot = lambda g0, g1: jnp.where(head0, g0[L:2 * L], g1[L:2 * L])
    join = lambda xs: xs[0] if gch == 1 else jnp.concatenate(xs, axis=1)
    for g in groups:
        pr, cg = g
        a_cat[g] = jnp.where(strict_cat, join([top(gram[ci, pr, 0], gram[ci, pr, 1]) for ci in members(cg)]), 0.0)
        b_cat[g] = jnp.where(strict_cat, join([top(gram[ci, pr, 1], gram[ci, pr, 0]) for ci in members(cg)]), 0.0)
        c_cat[g] = jnp.where(incl_cat, join([bot(gram[ci, pr, 0], gram[ci, pr, 1]) for ci in members(cg)]), 0.0)
        e_cat[g] = jnp.where(incl_cat, join([bot(gram[ci, pr, 1], gram[ci, pr, 0]) for ci in members(cg)]), 0.0)
    order_ac = (0, 1)
    order_be = (1, 0)

    n_pow = {g: -a_cat[g] for g in groups}
    t_inv = {g: eye_cat + n_pow[g] for g in groups}
    n_pow = {g: jnp.dot(n_pow[g].astype(BF16), block_diag(n_pow[g]), preferred_element_type=F32) for g in groups}
    for _ in range(L.bit_length() - 3):
        both = {g: jnp.dot(jnp.concatenate([n_pow[g], t_inv[g]], axis=0).astype(BF16), block_diag(n_pow[g]),
                           preferred_element_type=F32) for g in groups}
        n_pow = {g: both[g][0:L] for g in groups}
        t_inv = {g: t_inv[g] + both[g][L:2 * L] for g in groups}
    t_inv = {g: t_inv[g] + jnp.dot(t_inv[g].astype(BF16), block_diag(n_pow[g]), preferred_element_type=F32)
             for g in groups}

    def stacked(cg, order, part_fn):
        return jnp.concatenate([placed(ci, part_fn(ci, hh)) for ci in members(cg) for hh in order], axis=0)

    sec = lambda x, ci, width: x[:, (ci % gch) * width:(ci % gch + 1) * width]
    bv = {(pr, cg): jnp.dot(b_cat[pr, cg].astype(BF16),
                            stacked(cg, order_be, lambda ci, hh: [head_part(cut(v, ci, pr), hh)]),
                            preferred_element_type=F32) for pr, cg in groups}
    rows2 = lambda order, part_fn: jnp.concatenate(
        [jnp.concatenate(part_fn(hh), axis=1) for hh in order], axis=0)
    tr = {(ci, pr): jnp.dot(sec(t_inv[pr, ci // gch], ci, LANES).astype(BF16),
                            rows2(order_ac, lambda hh: [head_part(sec(bv[pr, ci // gch], ci, LANES), hh),
                                                        head_part(cut(kap_t, ci, pr), hh)]),
                            preferred_element_type=F32) for ci, pr in items}
    u0 = {it: -tr[it][:, 0:LANES] for it in items}
    wm = {it: tr[it][:, LANES:2 * LANES] for it in items}
    ce = {(ci, pr): jnp.dot(
        jnp.concatenate([sec(c_cat[pr, ci // gch], ci, LANES), sec(e_cat[pr, ci // gch], ci, LANES)],
                        axis=1).astype(BF16),
        jnp.concatenate([rows2(order_ac, lambda hh: [head_part(u0[ci, pr], hh), head_part(wm[ci, pr], hh)]),
                         rows2(order_be, lambda hh: [head_part(cut(v, ci, pr), hh), zeros_p])], axis=0),
        preferred_element_type=F32) for ci, pr in items}
    y0 = {it: ce[it][:, 0:LANES] for it in items}
    r_hat = {(ci, pr): cut(r_t, ci, pr) - ce[ci, pr][:, LANES:2 * LANES] for ci, pr in items}
    q_m = {it: -jnp.where(pair_diag, _mm_tn(wm[it], cut(beta_h, *it)), 0.0) for it in items}
    z_m = {it: jnp.where(pair_diag,
                         _mm_tn(jnp.concatenate([u0[it], cut(v, *it)], axis=0),
                                jnp.concatenate([cut(beta_h, *it), cut(k_h, *it)], axis=0)), 0.0)
           for it in items}

    s_mat = [s_scr[pr] for pr in range(npair)]
    y_rows = []
    for ci in range(nchunk):
        y_dep = [_mm_nt(r_hat[ci, pr], s_mat[pr]) for pr in range(npair)]
        s_q = [_mm(s_mat[pr], q_m[ci, pr]) for pr in range(npair)]
        s_mat = [s_mat[pr] * p_last[ci * L:ci * L + 1, pr * LANES:(pr + 1) * LANES] + s_q[pr] + z_m[ci, pr]
                 for pr in range(npair)]
        y_rows.append(jnp.concatenate([y_dep[pr] + y0[ci, pr] for pr in range(npair)], axis=-1))
    for pr in range(npair):
        s_scr[pr] = s_mat[pr]

    y = y_rows[0] if nchunk == 1 else jnp.concatenate(y_rows, axis=0)
    mu_y = _seg_sum(y, C_DH) * (1.0 / C_DH)
    yc = y - mu_y
    var = _seg_sum(yc * yc, C_DH) * (1.0 / C_DH)
    yn = yc * lax.rsqrt(var + RWKV_GN_EPS) * ln_w + ln_b
    bonus = _seg_sum(r * kc * r_k, C_DH) * v
    yc_ref[0] = (yn + bonus) * gate

    @pl.when(c == pl.num_programs(1) - 1)
    def _():
        sout_ref[0] = s_scr[...]
        shout_ref[0] = sh_scr[...]


def _rwkv(pc, sh0, s0, mu, vecs, w2p, a2p, g2p):
    bsz, t, _ = pc.shape
    L = min(RWKV_TILE, t)
    npair = C_HEADS // 2
    const = lambda b, c: (0, 0)
    return pl.pallas_call(
        functools.partial(_rwkv_body, nchunk=L // RWKV_CHUNK),
        grid=(bsz, t // L),
        in_specs=[pl.BlockSpec((1, L, C_COLS), lambda b, c: (b, c, 0)),
                  pl.BlockSpec((1, 1, C_COLS), lambda b, c: (b, 0, 0)),
                  pl.BlockSpec((1, npair, LANES, LANES), lambda b, c: (b, 0, 0, 0)),
                  pl.BlockSpec((1, C_COLS), const),
                  pl.BlockSpec((8, C_W), const),
                  pl.BlockSpec((C_LORA, C_W), const),
                  pl.BlockSpec((C_LORA, C_W), const),
                  pl.BlockSpec((C_LORA, C_W), const)],
        out_specs=[pl.BlockSpec((1, L, C_W), lambda b, c: (b, c, 0)),
                   pl.BlockSpec((1, npair, LANES, LANES), lambda b, c: (b, 0, 0, 0)),
                   pl.BlockSpec((1, 1, C_COLS), lambda b, c: (b, 0, 0))],
        out_shape=[jax.ShapeDtypeStruct((bsz, t, C_W), F32),
                   jax.ShapeDtypeStruct((bsz, npair, LANES, LANES), F32),
                   jax.ShapeDtypeStruct((bsz, 1, C_COLS), F32)],
        scratch_shapes=[pltpu.VMEM((npair, LANES, LANES), F32), pltpu.VMEM((1, C_COLS), F32)],
        compiler_params=_cparams("parallel", "arbitrary"),
        name="rwkv",
    )(pc, sh0, s0, mu, vecs, w2p, a2p, g2p)


def _outattn_body(x_ref, oa_ref, ob_ref, yc_ref, woa_ref, wob_ref, woc_ref, g_ref, wq_ref, wo_ref,
                  mk_ref, mv_ref, o_ref):
    tm, d = x_ref.shape[1], x_ref.shape[2]
    hd = d // MEM_HEADS
    subs = _row_subtiles(tm)
    heads = [slice(h * hd, (h + 1) * hd) for h in range(MEM_HEADS)]
    mk = mk_ref[0].astype(BF16)
    mv = mv_ref[0].astype(BF16)
    x = [x_ref[0, r, :] + _mm(oa_ref[0, r, :], woa_ref[...]) + _mm(ob_ref[0, r, :], wob_ref[...])
         + _mm(yc_ref[0, r, :], woc_ref[...]) for r in subs]
    q = [_mm(_rms(xi, g_ref[...]), wq_ref[...]) for xi in x]
    s = [[_mm_nt(qi[:, c], mk[:, c]) * (hd ** -0.5) for c in heads] for qi in q]
    o = []
    for si in s:
        e = [jnp.exp(sh - jnp.max(sh, axis=-1, keepdims=True)) for sh in si]
        pr = [eh * (1.0 / jnp.sum(eh, axis=-1, keepdims=True)) for eh in e]
        o.append([_mm(ph, mv[:, c]) for ph, c in zip(pr, heads)])
    for r, xi, oi in zip(subs, x, o):
        o_ref[0, r, :] = xi + _mm(jnp.concatenate(oi, axis=-1), wo_ref[...])


def _outattn(x, oa, ob, yc, woa, wob, woc, g, wq, wo, mk, mv, layer):
    bsz, t, d = x.shape
    tm = min(ATTN_TILE, t)
    n_mem = mk.shape[2]
    const = lambda b, i: (0, 0)
    tok = lambda w: pl.BlockSpec((1, tm, w), lambda b, i: (b, i, 0))
    full = lambda a: pl.BlockSpec(a.shape, const, pipeline_mode=pl.Buffered(1))
    mem = pl.BlockSpec((None, 1, n_mem, d), lambda b, i: (layer, b, 0, 0))
    return pl.pallas_call(
        _outattn_body,
        grid=(bsz, t // tm),
        in_specs=[tok(d), tok(A_W), tok(B_W), tok(C_W), full(woa), full(wob), full(woc), full(g),
                  full(wq), full(wo), mem, mem],
        out_specs=tok(d),
        out_shape=jax.ShapeDtypeStruct((bsz, t, d), F32),
        compiler_params=_cparams("parallel", "parallel"),
        name="outattn",
    )(x, oa, ob, yc, woa, wob, woc, g, wq, wo, mk, mv)


def _blockdiag_t(s):
    bsz, h, dk, dv = s.shape
    eye = jnp.eye(h, dtype=s.dtype)
    return jnp.einsum('bhkv,hg->bhvgk', s, eye).reshape(bsz, h * dv, h * dk)


def _unblockdiag_t(m, h, dk, dv):
    bsz = m.shape[0]
    m5 = m.reshape(bsz, h, dv, h, dk)
    return jnp.stack([m5[:, i, :, i, :] for i in range(h)], axis=1).transpose(0, 1, 3, 2)


def _pair_pack(s):
    bsz, h, d, _ = s.shape
    s4 = s.reshape(bsz, h // 2, 2, d, d)
    eye = jnp.eye(2, dtype=s.dtype)
    return jnp.einsum('bphvk,hg->bphvgk', s4, eye).reshape(bsz, h // 2, 2 * d, 2 * d)


def _pair_unpack(m):
    bsz, npair, w, _ = m.shape
    d = w // 2
    m6 = m.reshape(bsz, npair, 2, d, 2, d)
    return jnp.stack([m6[:, :, i, :, i, :] for i in range(2)], axis=2).reshape(bsz, 2 * npair, d, d)


def _rope_tables(pos):
    half = B_DK // 2
    inv = ROPE_BASE ** (-jnp.arange(half, dtype=F32) / half)
    ang = pos.astype(F32)[:, None] * inv[None, :]
    cos = jnp.tile(jnp.cos(ang), (1, 2 * 2 * B_HEADS))
    sin = jnp.tile(jnp.sin(ang), (1, 2 * 2 * B_HEADS))
    return cos, sin


def _prep_layer(l, W, lbs, log_gamma):
    row = lambda a: a.reshape(1, -1).astype(F32)
    w_in = W['w_in'][l]
    pad_rows = lambda m, off: jnp.zeros((C_LORA, C_W), F32).at[off:off + m.shape[0]].set(m).astype(BF16)
    lb = lbs[l]
    zero = jnp.zeros((C_W,), F32)
    vecs = jnp.stack([W['rwkv_w0'][l], W['rwkv_a0'][l], W['rwkv_k_k'][l], W['rwkv_k_a'][l],
                      W['rwkv_r_k'][l].reshape(-1), W['rwkv_ln_w'][l], W['rwkv_ln_b'][l], zero])
    lg_rows = jnp.stack([jnp.pad(jnp.repeat(log_gamma, B_DK), (0, B_W - B_KW)),
                         jnp.repeat(log_gamma, B_DV)])
    w_out = W['w_out'][l].astype(BF16)
    return dict(
        n1=row(W['norm_ffn1'][l]), g1=W['ffn1_wg'][l].astype(BF16), u1=W['ffn1_wu'][l].astype(BF16),
        d1=W['ffn1_wd'][l].astype(BF16),
        n2=row(W['norm_ffn2'][l]), g2=W['ffn2_wg'][l].astype(BF16), u2=W['ffn2_wu'][l].astype(BF16),
        d2=W['ffn2_wd'][l].astype(BF16),
        nmix=row(W['norm_mix'][l]),
        wa=w_in[:, :A_COLS].astype(BF16), wb=w_in[:, A_COLS:A_COLS + B_COLS].astype(BF16),
        wc=w_in[:, A_COLS + B_COLS:].astype(BF16),
        lb_floor=row(jnp.maximum(lb, LB_FLOOR)), one_minus_lb=row(1.0 - lb),
        gn=row(jnp.tile(W['hgrn_norm'][l], A_HEADS)),
        lg_rows=lg_rows, log_gamma=log_gamma,
        mu=row(W['rwkv_mu'][l]), vecs=vecs,
        w2p=pad_rows(W['rwkv_w2'][l], 0), a2p=pad_rows(W['rwkv_a2'][l], C_DECAY_RANK),
        g2p=pad_rows(W['rwkv_g2'][l], C_DECAY_RANK + C_AAA_RANK),
        woa=w_out[:A_W], wob=w_out[A_W:A_W + B_W], woc=w_out[A_W + B_W:],
        ncross=row(W['norm_cross'][l]), wq=W['wq_x'][l].astype(BF16), wo=W['wo_x'][l].astype(BF16),
    )


def _trunk(x, pos, mem_k, mem_v, st_a, st_b, st_c, st_sh, layers, final_g):
    bsz, t, d = x.shape
    n = bsz * t
    depth = len(layers)
    cos_t, sin_t = _rope_tables(pos)
    new_a, new_b, new_c, new_sh = [], [], [], []
    x2 = x.reshape(n, d)
    for l, p in enumerate(layers):
        x2 = _ffn(x2, p['n1'], p['g1'], p['u1'], p['d1'], final_g, False)
        pa, pb, pc = _inproj(x2, p['nmix'], p['wa'], p['wb'], p['wc'])
        oa, ma = _hgrn(pa.reshape(bsz, t, A_COLS), p['lb_floor'], p['one_minus_lb'], p['gn'],
                       _pair_pack(st_a[l].transpose(0, 1, 3, 2)))
        ob, mb = _retention(p['log_gamma'], pb.reshape(bsz, t, B_COLS), cos_t, sin_t, p['lg_rows'],
                            _blockdiag_t(st_b[l]))
        yc, sc, sh = _rwkv(pc.reshape(bsz, t, C_COLS), st_sh[l], _pair_pack(st_c[l]), p['mu'],
                           p['vecs'], p['w2p'], p['a2p'], p['g2p'])
        x3 = _outattn(x2.reshape(bsz, t, d), oa, ob, yc, p['woa'], p['wob'], p['woc'], p['ncross'],
                      p['wq'], p['wo'], mem_k.reshape(depth, bsz, -1, d), mem_v.reshape(depth, bsz, -1, d), l)
        x2 = _ffn(x3.reshape(n, d), p['n2'], p['g2'], p['u2'], p['d2'], final_g, l == depth - 1)
        new_a.append(_pair_unpack(ma).transpose(0, 1, 3, 2))
        new_b.append(_unblockdiag_t(mb, B_HEADS, B_DK, B_DV))
        new_c.append(_pair_unpack(sc))
        new_sh.append(sh)
    return (x2.reshape(bsz, t, d), jnp.stack(new_a), jnp.stack(new_b), jnp.stack(new_c),
            jnp.stack(new_sh))


def kernel(x_prompt, x_sample, mem_prompt, cache_mem_k, cache_mem_v, state_hgrn, state_ret, state_rwkv, state_rwkv_shift, norm_ffn1, ffn1_wg, ffn1_wu, ffn1_wd, norm_mix, w_in, hgrn_lb_param, hgrn_norm, rwkv_mu, rwkv_w0, rwkv_w2, rwkv_a0, rwkv_a2, rwkv_g2, rwkv_k_k, rwkv_k_a, rwkv_r_k, rwkv_ln_w, rwkv_ln_b, w_out, norm_cross, norm_mem, wq_x, wk_x, wv_x, wo_x, norm_ffn2, ffn2_wg, ffn2_wu, ffn2_wd, final_norm):
    W = dict(norm_ffn1=norm_ffn1, ffn1_wg=ffn1_wg, ffn1_wu=ffn1_wu, ffn1_wd=ffn1_wd, norm_mix=norm_mix,
             w_in=w_in, hgrn_norm=hgrn_norm, rwkv_mu=rwkv_mu, rwkv_w0=rwkv_w0, rwkv_w2=rwkv_w2,
             rwkv_a0=rwkv_a0, rwkv_a2=rwkv_a2, rwkv_g2=rwkv_g2, rwkv_k_k=rwkv_k_k, rwkv_k_a=rwkv_k_a,
             rwkv_r_k=rwkv_r_k, rwkv_ln_w=rwkv_ln_w, rwkv_ln_b=rwkv_ln_b, w_out=w_out,
             norm_cross=norm_cross, wq_x=wq_x, wo_x=wo_x, norm_ffn2=norm_ffn2, ffn2_wg=ffn2_wg,
             ffn2_wu=ffn2_wu, ffn2_wd=ffn2_wd)
    depth = w_in.shape[0]
    d = x_prompt.shape[-1]
    sm = jax.nn.softmax(hgrn_lb_param.astype(F32), axis=0)
    lbs = jnp.cumsum(sm, axis=0) - sm[0:1]
    log_gamma = jnp.log1p(-jnp.exp2(-5.0 - jnp.arange(B_HEADS, dtype=F32)))
    layers = [_prep_layer(l, W, lbs, log_gamma) for l in range(depth)]
    final_g = final_norm.reshape(1, -1).astype(F32)

    bp, tp, _ = x_prompt.shape
    n_mem = mem_prompt.shape[1]
    hd = d // MEM_HEADS
    mem2 = mem_prompt.reshape(bp * n_mem, d)
    mks, mvs = [], []
    for l in range(depth):
        mk, mv = _memkv(mem2, norm_mem[l].reshape(1, -1), wk_x[l].astype(BF16), wv_x[l].astype(BF16))
        mks.append(mk.reshape(bp, n_mem, MEM_HEADS, hd))
        mvs.append(mv.reshape(bp, n_mem, MEM_HEADS, hd))
    p_mem_k = jnp.stack(mks)
    p_mem_v = jnp.stack(mvs)

    zeros = lambda *s: jnp.zeros((depth, bp) + s, F32)
    y_prompt, p_hgrn, p_ret, p_rwkv, p_shift = _trunk(
        x_prompt, jnp.arange(tp, dtype=jnp.int32), p_mem_k, p_mem_v,
        zeros(A_HEADS, A_DK, A_DV), zeros(B_HEADS, B_DK, B_DV), zeros(C_HEADS, C_DH, C_DH),
        zeros(1, C_COLS), layers, final_g)

    ts = x_sample.shape[1]
    y_sample, s_hgrn, s_ret, s_rwkv, s_shift = _trunk(
        x_sample, PAST_LEN + jnp.arange(ts, dtype=jnp.int32), cache_mem_k, cache_mem_v,
        state_hgrn, state_ret, state_rwkv, state_rwkv_shift, layers, final_g)

    return (y_prompt, y_sample, p_hgrn, p_ret, p_rwkv, p_shift, p_mem_k, p_mem_v,
            s_hgrn, s_ret, s_rwkv, s_shift)
```

```python
import functools

import jax
import jax.numpy as jnp
from jax import lax
from jax.experimental import pallas as pl
from jax.experimental.pallas import tpu as pltpu

F32 = jnp.float32
BF16 = jnp.bfloat16

EPS = 1e-6
LB_FLOOR = 1e-20
ROPE_BASE = 10000.0
PAST_LEN = 4096
MEM_HEADS = 4
A_HEADS, A_DK, A_DV = 4, 64, 64
A_KW = A_HEADS * A_DK
A_W = A_HEADS * A_DV
A_COLS = 2 * A_KW + 2 * A_W
B_HEADS, B_DK, B_DV = 6, 32, 64
B_KW = B_HEADS * B_DK
B_W = B_HEADS * B_DV
B_COLS = 2 * B_KW + 2 * B_W
C_HEADS, C_DH = 6, 64
C_W = C_HEADS * C_DH
C_DECAY_RANK, C_AAA_RANK, C_GATE_RANK = 32, 32, 64
C_LORA = C_DECAY_RANK + C_AAA_RANK + C_GATE_RANK
C_COLS = 3 * C_W + C_LORA
RWKV_GN_EPS = 64e-5
LOG2E = 1.4426950408889634

LANES = 128
VMEM_LIMIT = 56 * 1024 * 1024
FFN_TILE = 1024
FFN_FCHUNK = 256
TOK_TILE = 512
MIXER_TILE = 256
HGRN_BLOCK = 16
RWKV_CHUNK = 64
RWKV_GROUP_CHUNKS = 2


def _cparams(*sem):
    return pltpu.CompilerParams(dimension_semantics=sem, vmem_limit_bytes=VMEM_LIMIT)


def _mm(a, b):
    return jnp.dot(a.astype(BF16), b.astype(BF16), preferred_element_type=F32)


def _mm_nt(a, b):
    return lax.dot_general(a.astype(BF16), b.astype(BF16), (((1,), (1,)), ((), ())),
                           preferred_element_type=F32)


def _mm_tn(a, b):
    return lax.dot_general(a.astype(BF16), b.astype(BF16), (((0,), (0,)), ((), ())),
                           preferred_element_type=F32)


def _split3(x):
    hi = x.astype(BF16)
    r = x - hi.astype(F32)
    mid = r.astype(BF16)
    lo = (r - mid.astype(F32)).astype(BF16)
    return hi, mid, lo


def _mm_exact_lhs(c, x):
    hi, mid, lo = _split3(x)
    return (jnp.dot(c, hi, preferred_element_type=F32) + jnp.dot(c, mid, preferred_element_type=F32)
            + jnp.dot(c, lo, preferred_element_type=F32))


def _rms(x, g):
    return x * lax.rsqrt(jnp.mean(x * x, axis=-1, keepdims=True) + EPS) * g


def _sigmoid(x):
    return 0.5 * jnp.tanh(0.5 * x) + 0.5


def _silu(x):
    return x * _sigmoid(x)


def _softplus(x):
    return jnp.maximum(x, 0.0) + jnp.log(1.0 + jnp.exp(-jnp.abs(x)))


def _iota2(shape, axis):
    return lax.broadcasted_iota(jnp.int32, shape, axis)


def _block_ones(n, blk):
    return (_iota2((n, n), 0) // blk) == (_iota2((n, n), 1) // blk)


def _seg_sum(x, blk):
    n = x.shape[-1]
    outs = []
    for lo in range(0, n, LANES):
        w = min(LANES, n - lo)
        outs.append(jnp.dot(x[:, lo:lo + w].astype(BF16), _block_ones(w, blk).astype(BF16),
                            preferred_element_type=F32))
    return outs[0] if len(outs) == 1 else jnp.concatenate(outs, axis=-1)


def _ffn_body(x_ref, g_ref, wg_ref, wu_ref, wd_ref, gf_ref, o_ref, *, apply_final_norm):
    x = x_ref[...]
    hb = _rms(x, g_ref[...]).astype(BF16)
    d_ff = wg_ref.shape[1]
    acc = jnp.zeros(x.shape, F32)
    for j in range(d_ff // FFN_FCHUNK):
        cols = slice(j * FFN_FCHUNK, (j + 1) * FFN_FCHUNK)
        a = jnp.dot(hb, wg_ref[:, cols], preferred_element_type=F32)
        u = jnp.dot(hb, wu_ref[:, cols], preferred_element_type=F32)
        t = (_silu(a) * u).astype(BF16)
        acc = acc + jnp.dot(t, wd_ref[cols, :], preferred_element_type=F32)
    y = x + 0.5 * acc
    if apply_final_norm:
        y = _rms(y, gf_ref[...])
    o_ref[...] = y


def _ffn(x, g, wg, wu, wd, gf, apply_final_norm):
    n, d = x.shape
    d_ff = wg.shape[1]
    tm = min(FFN_TILE, n)
    const = lambda i: (0, 0)
    resident = lambda shape: pl.BlockSpec(shape, const, pipeline_mode=pl.Buffered(1))
    return pl.pallas_call(
        functools.partial(_ffn_body, apply_final_norm=apply_final_norm),
        grid=(n // tm,),
        in_specs=[pl.BlockSpec((tm, d), lambda i: (i, 0)),
                  pl.BlockSpec((1, d), const),
                  resident((d, d_ff)),
                  resident((d, d_ff)),
                  resident((d_ff, d)),
                  pl.BlockSpec((1, d), const)],
        out_specs=pl.BlockSpec((tm, d), lambda i: (i, 0)),
        out_shape=jax.ShapeDtypeStruct((n, d), F32),
        compiler_params=_cparams("parallel"),
        name="ffn",
    )(x, g, wg, wu, wd, gf)


def _memkv_body(m_ref, g_ref, wk_ref, wv_ref, k_ref, v_ref):
    hb = _rms(m_ref[...], g_ref[...]).astype(BF16)
    k_ref[...] = jnp.dot(hb, wk_ref[...], preferred_element_type=F32)
    v_ref[...] = jnp.dot(hb, wv_ref[...], preferred_element_type=F32)


def _memkv(mem, g, wk, wv):
    n, d = mem.shape
    tm = min(TOK_TILE, n)
    const = lambda i: (0, 0)
    row = lambda i: (i, 0)
    return pl.pallas_call(
        _memkv_body,
        grid=(n // tm,),
        in_specs=[pl.BlockSpec((tm, d), row), pl.BlockSpec((1, d), const),
                  pl.BlockSpec((d, d), const), pl.BlockSpec((d, d), const)],
        out_specs=[pl.BlockSpec((tm, d), row), pl.BlockSpec((tm, d), row)],
        out_shape=[jax.ShapeDtypeStruct((n, d), F32)] * 2,
        compiler_params=_cparams("parallel"),
        name="memkv",
    )(mem, g, wk, wv)


def _hgrn_tile(x, lbf_ref, oml_ref, gn_ref, m_scr):
    tile = x.shape[0]
    blk = HGRN_BLOCK
    nb = tile // blk
    q = x[:, 0:A_KW]
    f = x[:, A_KW:2 * A_KW]
    v = x[:, 2 * A_KW:2 * A_KW + A_W]
    g = x[:, 2 * A_KW + A_W:A_COLS]
    gate_f = lbf_ref[...] + oml_ref[...] * _sigmoid(f)
    logf = jnp.log(gate_f)
    ka = 1.0 - gate_f
    qa = _silu(q) * (A_DK ** -0.5)
    ti = _iota2((tile, tile), 0)
    si = _iota2((tile, tile), 1)
    same_blk = (ti // blk) == (si // blk)
    b = _mm_exact_lhs((same_blk & (ti >= si)).astype(BF16), logf)
    b_tot = jnp.broadcast_to(b.reshape(nb, blk, A_KW)[:, blk - 1:blk, :],
                             (nb, blk, A_KW)).reshape(tile, A_KW)
    q_t = qa * jnp.exp(b)
    k_hat = ka * jnp.exp(b_tot - b)
    d_blk = jnp.exp(b_tot)
    head_ones = _block_ones(A_KW, A_DK)
    head_ones_b = head_ones.astype(BF16)

    hb = blk // 2
    halves = lambda a: (lambda a4: (a4[:, 0], a4[:, 1]))(a.reshape(nb, 2, hb, A_KW))
    b_h, ka_h, qa_h, v_h = halves(b * LOG2E), halves(ka), halves(qa), halves(v)
    src_row = lambda a_h, s: a_h[s // hb][:, s % hb:s % hb + 1, :]
    row_h = lax.broadcasted_iota(jnp.int32, (nb, hb, A_KW), 1)
    slab_keys, slabs = [], []
    for s in range(blk):
        for half in range(s // hb, 2):
            z = qa_h[half] * src_row(ka_h, s) * jnp.exp2(jnp.minimum(b_h[half] - src_row(b_h, s), 0.0))
            if half == s // hb:
                z = jnp.where(row_h >= s % hb, z, 0.0)
            slab_keys.append((s, half))
            slabs.append(z.reshape(nb * hb, A_KW))
    att = _mm(jnp.concatenate(slabs, axis=0), head_ones_b)
    o_h = [jnp.zeros((nb, hb, A_W), F32), jnp.zeros((nb, hb, A_W), F32)]
    for i, (s, half) in enumerate(slab_keys):
        part = att[i * nb * hb:(i + 1) * nb * hb, :].reshape(nb, hb, A_W)
        o_h[half] = o_h[half] + part * src_row(v_h, s)
    o3 = jnp.stack(o_h, axis=1)

    npair = A_HEADS // 2
    pair_diag = _block_ones(LANES, A_DK)
    part = lambda a, j, p: a[j * blk:(j + 1) * blk, p * LANES:(p + 1) * LANES]
    outer = [[jnp.where(pair_diag, _mm_tn(part(v, j, p), part(k_hat, j, p)), 0.0) for p in range(npair)]
             for j in range(nb)]
    m = [m_scr[p] for p in range(npair)]
    states = []
    for j in range(nb):
        states.append(m)
        m = [m[p] * d_blk[j * blk:j * blk + 1, p * LANES:(p + 1) * LANES] + outer[j][p] for p in range(npair)]
    for p in range(npair):
        m_scr[p] = m[p]
    o = jnp.concatenate(
        [jnp.concatenate([_mm_nt(part(q_t, j, p), states[j][p]) for p in range(npair)], axis=1)
         for j in range(nb)], axis=0)
    o = o + o3.reshape(tile, A_W)
    ms = _mm(o * o, head_ones_b) * (1.0 / A_DV)
    return o * lax.rsqrt(ms + EPS) * gn_ref[...] * _silu(g)


def _ret_tables(lgs_ref, lg_ref, d_scr, eq_scr, ev_scr):
    tile = eq_scr.shape[0]
    rel = (_iota2((tile, tile), 0) - _iota2((tile, tile), 1)).astype(F32)
    for h in range(B_HEADS):
        d_scr[h] = jnp.where(rel >= 0.0, jnp.exp(lgs_ref[h] * jnp.maximum(rel, 0.0)), 0.0)
    j_k = _iota2((tile, B_KW), 0).astype(F32)
    j_v = _iota2((tile, B_W), 0).astype(F32)
    eq_scr[...] = jnp.exp(lg_ref[0:1, 0:B_KW] * (j_k + 1.0))
    ev_scr[...] = jnp.exp(lg_ref[1:2, :] * (tile - 1.0 - j_v))


def _ret_tile(x, cos_ref, sin_ref, lg_ref, m_scr, d_scr, eq_scr, ev_scr):
    tile = x.shape[0]
    qk = x[:, 0:2 * B_KW]
    v = x[:, 2 * B_KW:2 * B_KW + B_W]
    g = x[:, 2 * B_KW + B_W:B_COLS]
    half = B_DK // 2
    first_half = (_iota2(qk.shape, 1) % B_DK) < half
    swapped = jnp.where(first_half, -pltpu.roll(qk, 2 * B_KW - half, 1), pltpu.roll(qk, half, 1))
    qk = qk * cos_ref[...] + swapped * sin_ref[...]
    q = qk[:, 0:B_KW]
    k = qk[:, B_KW:2 * B_KW] * (B_DK ** -0.5)

    lg_k = lg_ref[0:1, 0:B_KW]
    m = m_scr[...]
    cross = _mm_nt(q * eq_scr[...], m)
    scores = [_mm_nt(q[:, h * B_DK:(h + 1) * B_DK], k[:, h * B_DK:(h + 1) * B_DK]) for h in range(B_HEADS)]
    outs = [_mm(scores[h] * d_scr[h], v[:, h * B_DV:(h + 1) * B_DV]) for h in range(B_HEADS)]
    o = jnp.concatenate(outs, axis=-1) + cross
    mu = _seg_sum(o, B_DV) * (1.0 / B_DV)
    oc = o - mu
    var = _seg_sum(oc * oc, B_DV) * (1.0 / B_DV)
    v_w = v * ev_scr[...]
    head_mask = (_iota2((B_W, B_KW), 0) // B_DV) == (_iota2((B_W, B_KW), 1) // B_DK)
    m_scr[...] = m * jnp.exp(lg_k * float(tile)) + jnp.where(head_mask, _mm_tn(v_w, k), 0.0)
    return oc * lax.rsqrt(var + EPS) * _silu(g)


def _rwkv_tile(p, mu_ref, vec_ref, w2_ref, a2_ref, g2_ref, s_scr, sh_scr):
    L = RWKV_CHUNK
    tl = p.shape[0]
    nchunk = tl // L
    npair = C_HEADS // 2
    row = _iota2((tl, C_COLS), 0)
    prev = jnp.where(row == 0, sh_scr[...], pltpu.roll(p, 1, 0))
    xs = p + mu_ref[...] * (prev - p)
    sh_scr[...] = p[tl - 1:tl, :]

    r = xs[:, 0:C_W]
    k = xs[:, C_W:2 * C_W]
    v = xs[:, 2 * C_W:3 * C_W]
    lora = xs[:, 3 * C_W:C_COLS]
    w0, a0, k_k, k_a, r_k, ln_w, ln_b = (vec_ref[i:i + 1, :] for i in range(7))
    w = -_softplus(-(w0 + _mm(jnp.tanh(lora), w2_ref[...]))) - 0.5
    lw = -jnp.exp(w)
    a = _sigmoid(a0 + _mm(lora, a2_ref[...]))
    gate = _mm(_sigmoid(lora), g2_ref[...])
    kk = k * k_k
    kk = kk / jnp.maximum(jnp.sqrt(_seg_sum(kk * kk, C_DH)), 1e-6)
    kc = k * (1.0 + (a - 1.0) * k_a)
    beta = kk * a

    ti = _iota2((tl, tl), 0)
    si = _iota2((tl, tl), 1)
    same_chunk = (ti // L) == (si // L)
    b = _mm_exact_lhs((same_chunk & (ti >= si)).astype(BF16), lw)
    b_tot = jnp.broadcast_to(b.reshape(nchunk, L, C_W)[:, L - 1:L, :],
                             (nchunk, L, C_W)).reshape(tl, C_W)
    e_nb = jnp.exp(-b)
    e_tail = jnp.exp(b_tot - b)
    kap_t = kk * jnp.exp(b - lw)
    r_t = r * jnp.exp(b)
    beta_t = beta * e_nb
    k_t = kc * e_nb
    beta_h = beta * e_tail
    k_h = kc * e_tail
    p_last = jnp.exp(b_tot)

    gch = min(RWKV_GROUP_CHUNKS, nchunk)
    nblk = 2 * gch
    wcat = L * nblk
    t_cat = _iota2((L, wcat), 0)
    s_cat = _iota2((L, wcat), 1) % L
    blk_cat = _iota2((L, wcat), 1) // L
    strict_cat = t_cat > s_cat
    incl_cat = t_cat >= s_cat
    eye_cat = (t_cat == s_cat).astype(F32)
    head0 = (_iota2((L, LANES), 1) // C_DH) == 0
    hmask = [head0, jnp.logical_not(head0)]
    pair_diag = _block_ones(LANES, C_DH)
    zeros_p = jnp.zeros((L, LANES), BF16)

    cut = lambda x, ci, pr: x[ci * L:(ci + 1) * L, pr * LANES:(pr + 1) * LANES]
    items = [(ci, pr) for ci in range(nchunk) for pr in range(npair)]
    groups = [(pr, cg) for cg in range(nchunk // gch) for pr in range(npair)]
    members = lambda cg: range(cg * gch, (cg + 1) * gch)

    def block_diag(x):
        xb = x.astype(BF16)
        return jnp.concatenate([jnp.where(blk_cat == j, xb, jnp.zeros_like(xb)) for j in range(nblk)], axis=0)

    def placed(ci, parts):
        width = len(parts) * LANES
        body = parts[0] if len(parts) == 1 else jnp.concatenate(parts, axis=1)
        secs = [body if cj == ci % gch else jnp.zeros((L, width), BF16) for cj in range(gch)]
        return secs[0] if gch == 1 else jnp.concatenate(secs, axis=1)

    def head_part(x, hh):
        return jnp.where(hmask[hh], x.astype(BF16), zeros_p)

    a_cat, b_cat, c_cat, e_cat = {}, {}, {}, {}
    gram = {}
    for ci, pr in items:
        lhs = jnp.concatenate([jnp.where(hmask[hh], cut(x, ci, pr), 0.0)
                               for hh in range(2) for x in (kap_t, r_t)], axis=0)
        both = _mm_nt(lhs, jnp.concatenate([cut(beta_t, ci, pr), cut(k_t, ci, pr)], axis=0))
        gram[ci, pr, 0] = both[0:2 * L]
        gram[ci, pr, 1] = pltpu.roll(both[2 * L:4 * L], LANES // 2, 1)
    top = lambda g0, g1: jnp.where(head0, g0[0:L], g1[0:L])
    bot = lambda g0, g1: jnp.where(head0, g0[L:2 * L], g1[L:2 * L])
    join = lambda xs: xs[0] if gch == 1 else jnp.concatenate(xs, axis=1)
    for g in groups:
        pr, cg = g
        a_cat[g] = jnp.where(strict_cat, join([top(gram[ci, pr, 0], gram[ci, pr, 1]) for ci in members(cg)]), 0.0)
        b_cat[g] = jnp.where(strict_cat, join([top(gram[ci, pr, 1], gram[ci, pr, 0]) for ci in members(cg)]), 0.0)
        c_cat[g] = jnp.where(incl_cat, join([bot(gram[ci, pr, 0], gram[ci, pr, 1]) for ci in members(cg)]), 0.0)
        e_cat[g] = jnp.where(incl_cat, join([bot(gram[ci, pr, 1], gram[ci, pr, 0]) for ci in members(cg)]), 0.0)
    order_ac = (0, 1)
    order_be = (1, 0)

    n_pow = {g: -a_cat[g] for g in groups}
    t_inv = {g: eye_cat + n_pow[g] for g in groups}
    n_pow = {g: jnp.dot(n_pow[g].astype(BF16), block_diag(n_pow[g]), preferred_element_type=F32) for g in groups}
    for _ in range(L.bit_length() - 3):
        both = {g: jnp.dot(jnp.concatenate([n_pow[g], t_inv[g]], axis=0).astype(BF16), block_diag(n_pow[g]),
                           preferred_element_type=F32) for g in groups}
        n_pow = {g: both[g][0:L] for g in groups}
        t_inv = {g: t_inv[g] + both[g][L:2 * L] for g in groups}
    t_inv = {g: t_inv[g] + jnp.dot(t_inv[g].astype(BF16), block_diag(n_pow[g]), preferred_element_type=F32)
             for g in groups}

    def stacked(cg, order, part_fn):
        return jnp.concatenate([placed(ci, part_fn(ci, hh)) for ci in members(cg) for hh in order], axis=0)

    sec = lambda x, ci, width: x[:, (ci % gch) * width:(ci % gch + 1) * width]
    bv = {(pr, cg): jnp.dot(b_cat[pr, cg].astype(BF16),
                            stacked(cg, order_be, lambda ci, hh: [head_part(cut(v, ci, pr), hh)]),
                            preferred_element_type=F32) for pr, cg in groups}
    rows2 = lambda order, part_fn: jnp.concatenate(
        [jnp.concatenate(part_fn(hh), axis=1) for hh in order], axis=0)
    tr = {(ci, pr): jnp.dot(sec(t_inv[pr, ci // gch], ci, LANES).astype(BF16),
                            rows2(order_ac, lambda hh: [head_part(sec(bv[pr, ci // gch], ci, LANES), hh),
                                                        head_part(cut(kap_t, ci, pr), hh)]),
                            preferred_element_type=F32) for ci, pr in items}
    u0 = {it: -tr[it][:, 0:LANES] for it in items}
    wm = {it: tr[it][:, LANES:2 * LANES] for it in items}
    ce = {(ci, pr): jnp.dot(
        jnp.concatenate([sec(c_cat[pr, ci // gch], ci, LANES), sec(e_cat[pr, ci // gch], ci, LANES)],
                        axis=1).astype(BF16),
        jnp.concatenate([rows2(order_ac, lambda hh: [head_part(u0[ci, pr], hh), head_part(wm[ci, pr], hh)]),
                         rows2(order_be, lambda hh: [head_part(cut(v, ci, pr), hh), zeros_p])], axis=0),
        preferred_element_type=F32) for ci, pr in items}
    y0 = {it: ce[it][:, 0:LANES] for it in items}
    r_hat = {(ci, pr): cut(r_t, ci, pr) - ce[ci, pr][:, LANES:2 * LANES] for ci, pr in items}
    q_m = {it: -jnp.where(pair_diag, _mm_tn(wm[it], cut(beta_h, *it)), 0.0) for it in items}
    z_m = {it: jnp.where(pair_diag,
                         _mm_tn(jnp.concatenate([u0[it], cut(v, *it)], axis=0),
                                jnp.concatenate([cut(beta_h, *it), cut(k_h, *it)], axis=0)), 0.0)
           for it in items}

    s_mat = [s_scr[pr] for pr in range(npair)]
    y_rows = []
    for ci in range(nchunk):
        y_dep = [_mm_nt(r_hat[ci, pr], s_mat[pr]) for pr in range(npair)]
        s_q = [_mm(s_mat[pr], q_m[ci, pr]) for pr in range(npair)]
        s_mat = [s_mat[pr] * p_last[ci * L:ci * L + 1, pr * LANES:(pr + 1) * LANES] + s_q[pr] + z_m[ci, pr]
                 for pr in range(npair)]
        y_rows.append(jnp.concatenate([y_dep[pr] + y0[ci, pr] for pr in range(npair)], axis=-1))
    for pr in range(npair):
        s_scr[pr] = s_mat[pr]

    y = y_rows[0] if nchunk == 1 else jnp.concatenate(y_rows, axis=0)
    mu_y = _seg_sum(y, C_DH) * (1.0 / C_DH)
    yc = y - mu_y
    var = _seg_sum(yc * yc, C_DH) * (1.0 / C_DH)
    yn = yc * lax.rsqrt(var + RWKV_GN_EPS) * ln_w + ln_b
    bonus = _seg_sum(r * kc * r_k, C_DH) * v
    return (yn + bonus) * gate


def _mixer_body(lgs_ref, x_ref, gmix_ref, wa_ref, wb_ref, wc_ref,
                lbf_ref, oml_ref, gn_ref, ma0_ref,
                cos_ref, sin_ref, lg_ref, mb0_ref,
                sh0_ref, s0_ref, mu_ref, vec_ref, w2_ref, a2_ref, g2_ref,
                woa_ref, wob_ref, woc_ref, gx_ref, wq_ref, wo_ref, mk_ref, mv_ref,
                o_ref, maout_ref, mbout_ref, sout_ref, shout_ref,
                ma_scr, mb_scr, d_scr, eq_scr, ev_scr, s_scr, sh_scr, mk_scr, mv_scr):
    c = pl.program_id(1)

    @pl.when(c == 0)
    def _():
        ma_scr[...] = ma0_ref[0]
        mb_scr[...] = mb0_ref[0]
        s_scr[...] = s0_ref[0]
        sh_scr[...] = sh0_ref[0]
        _ret_tables(lgs_ref, lg_ref, d_scr, eq_scr, ev_scr)
        mk_scr[...] = mk_ref[0].astype(BF16)
        mv_scr[...] = mv_ref[0].astype(BF16)

    x = x_ref[0]
    hb = _rms(x, gmix_ref[...]).astype(BF16)
    pa = jnp.dot(hb, wa_ref[...], preferred_element_type=F32)
    pb = jnp.dot(hb, wb_ref[...], preferred_element_type=F32)
    pc = jnp.dot(hb, wc_ref[...], preferred_element_type=F32)
    oa = _hgrn_tile(pa, lbf_ref, oml_ref, gn_ref, ma_scr)
    ob = _ret_tile(pb, cos_ref, sin_ref, lg_ref, mb_scr, d_scr, eq_scr, ev_scr)
    yc = _rwkv_tile(pc, mu_ref, vec_ref, w2_ref, a2_ref, g2_ref, s_scr, sh_scr)

    x = x + _mm(oa, woa_ref[...]) + _mm(ob, wob_ref[...]) + _mm(yc, woc_ref[...])
    q = _mm(_rms(x, gx_ref[...]), wq_ref[...])
    hd = x.shape[-1] // MEM_HEADS
    heads = [slice(h * hd, (h + 1) * hd) for h in range(MEM_HEADS)]
    s = [_mm_nt(q[:, hc], mk_scr[:, hc]) * (hd ** -0.5) for hc in heads]
    e = [jnp.exp(sh - jnp.max(sh, axis=-1, keepdims=True)) for sh in s]
    pr = [eh * (1.0 / jnp.sum(eh, axis=-1, keepdims=True)) for eh in e]
    att = [_mm(ph, mv_scr[:, hc]) for ph, hc in zip(pr, heads)]
    o_ref[0] = x + _mm(jnp.concatenate(att, axis=-1), wo_ref[...])

    @pl.when(c == pl.num_programs(1) - 1)
    def _():
        maout_ref[0] = ma_scr[...]
        mbout_ref[0] = mb_scr[...]
        sout_ref[0] = s_scr[...]
        shout_ref[0] = sh_scr[...]


def _mixer(x, p, cos_t, sin_t, ma0, mb0, sh0, s0, mk, mv, layer):
    bsz, t, d = x.shape
    tile = min(MIXER_TILE, t)
    n_mem = mk.shape[2]
    mem = pl.BlockSpec((None, 1, n_mem, d), lambda b, c: (layer, b, 0, 0))
    npa, npc = A_HEADS // 2, C_HEADS // 2
    const = lambda b, c: (0, 0)
    tok = lambda w: pl.BlockSpec((1, tile, w), lambda b, c: (b, c, 0))
    full = lambda a: pl.BlockSpec(a.shape, const, pipeline_mode=pl.Buffered(1))
    pos = lambda w: pl.BlockSpec((tile, w), lambda b, c: (c, 0))
    per_batch = lambda *s: pl.BlockSpec((1,) + s, lambda b, c: (b,) + (0,) * len(s))
    ins = [(p['log_gamma'], pl.BlockSpec(memory_space=pltpu.SMEM)),
           (x, tok(d)), (p['nmix'], full(p['nmix'])),
           (p['wa'], full(p['wa'])), (p['wb'], full(p['wb'])), (p['wc'], full(p['wc'])),
           (p['lb_floor'], full(p['lb_floor'])), (p['one_minus_lb'], full(p['one_minus_lb'])),
           (p['gn'], full(p['gn'])), (ma0, per_batch(npa, LANES, LANES)),
           (cos_t, pos(2 * B_KW)), (sin_t, pos(2 * B_KW)), (p['lg_rows'], full(p['lg_rows'])),
           (mb0, per_batch(B_W, B_KW)),
           (sh0, per_batch(1, C_COLS)), (s0, per_batch(npc, LANES, LANES)),
           (p['mu'], full(p['mu'])), (p['vecs'], full(p['vecs'])),
           (p['w2p'], full(p['w2p'])), (p['a2p'], full(p['a2p'])), (p['g2p'], full(p['g2p'])),
           (p['woa'], full(p['woa'])), (p['wob'], full(p['wob'])), (p['woc'], full(p['woc'])),
           (p['ncross'], full(p['ncross'])), (p['wq'], full(p['wq'])), (p['wo'], full(p['wo'])),
           (mk, mem), (mv, mem)]
    outs = [((bsz, t, d), tok(d)),
            ((bsz, npa, LANES, LANES), per_batch(npa, LANES, LANES)),
            ((bsz, B_W, B_KW), per_batch(B_W, B_KW)),
            ((bsz, npc, LANES, LANES), per_batch(npc, LANES, LANES)),
            ((bsz, 1, C_COLS), per_batch(1, C_COLS))]
    return pl.pallas_call(
        _mixer_body,
        grid=(bsz, t // tile),
        in_specs=[s for _, s in ins],
        out_specs=[s for _, s in outs],
        out_shape=[jax.ShapeDtypeStruct(shape, F32) for shape, _ in outs],
        scratch_shapes=[pltpu.VMEM((npa, LANES, LANES), F32), pltpu.VMEM((B_W, B_KW), F32),
                        pltpu.VMEM((B_HEADS, tile, tile), F32), pltpu.VMEM((tile, B_KW), F32),
                        pltpu.VMEM((tile, B_W), F32),
                        pltpu.VMEM((npc, LANES, LANES), F32), pltpu.VMEM((1, C_COLS), F32),
                        pltpu.VMEM((n_mem, d), BF16), pltpu.VMEM((n_mem, d), BF16)],
        compiler_params=_cparams("parallel", "arbitrary"),
        name="mixer",
    )(*[a for a, _ in ins])


def _blockdiag_t(s):
    bsz, h, dk, dv = s.shape
    eye = jnp.eye(h, dtype=s.dtype)
    return jnp.einsum('bhkv,hg->bhvgk', s, eye).reshape(bsz, h * dv, h * dk)


def _unblockdiag_t(m, h, dk, dv):
    bsz = m.shape[0]
    m5 = m.reshape(bsz, h, dv, h, dk)
    return jnp.stack([m5[:, i, :, i, :] for i in range(h)], axis=1).transpose(0, 1, 3, 2)


def _pair_pack(s):
    bsz, h, d, _ = s.shape
    s4 = s.reshape(bsz, h // 2, 2, d, d)
    eye = jnp.eye(2, dtype=s.dtype)
    return jnp.einsum('bphvk,hg->bphvgk', s4, eye).reshape(bsz, h // 2, 2 * d, 2 * d)


def _pair_unpack(m):
    bsz, npair, w, _ = m.shape
    d = w // 2
    m6 = m.reshape(bsz, npair, 2, d, 2, d)
    return jnp.stack([m6[:, :, i, :, i, :] for i in range(2)], axis=2).reshape(bsz, 2 * npair, d, d)


def _rope_tables(pos):
    half = B_DK // 2
    inv = ROPE_BASE ** (-jnp.arange(half, dtype=F32) / half)
    ang = pos.astype(F32)[:, None] * inv[None, :]
    cos = jnp.tile(jnp.cos(ang), (1, 2 * 2 * B_HEADS))
    sin = jnp.tile(jnp.sin(ang), (1, 2 * 2 * B_HEADS))
    return cos, sin


def _prep_layer(l, W, lbs, log_gamma):
    row = lambda a: a.reshape(1, -1).astype(F32)
    w_in = W['w_in'][l]
    pad_rows = lambda m, off: jnp.zeros((C_LORA, C_W), F32).at[off:off + m.shape[0]].set(m).astype(BF16)
    lb = lbs[l]
    zero = jnp.zeros((C_W,), F32)
    vecs = jnp.stack([W['rwkv_w0'][l], W['rwkv_a0'][l], W['rwkv_k_k'][l], W['rwkv_k_a'][l],
                      W['rwkv_r_k'][l].reshape(-1), W['rwkv_ln_w'][l], W['rwkv_ln_b'][l], zero])
    lg_rows = jnp.stack([jnp.pad(jnp.repeat(log_gamma, B_DK), (0, B_W - B_KW)),
                         jnp.repeat(log_gamma, B_DV)])
    w_out = W['w_out'][l].astype(BF16)
    return dict(
        n1=row(W['norm_ffn1'][l]), g1=W['ffn1_wg'][l].astype(BF16), u1=W['ffn1_wu'][l].astype(BF16),
        d1=W['ffn1_wd'][l].astype(BF16),
        n2=row(W['norm_ffn2'][l]), g2=W['ffn2_wg'][l].astype(BF16), u2=W['ffn2_wu'][l].astype(BF16),
        d2=W['ffn2_wd'][l].astype(BF16),
        nmix=row(W['norm_mix'][l]),
        wa=w_in[:, :A_COLS].astype(BF16), wb=w_in[:, A_COLS:A_COLS + B_COLS].astype(BF16),
        wc=w_in[:, A_COLS + B_COLS:].astype(BF16),
        lb_floor=row(jnp.maximum(lb, LB_FLOOR)), one_minus_lb=row(1.0 - lb),
        gn=row(jnp.tile(W['hgrn_norm'][l], A_HEADS)),
        lg_rows=lg_rows, log_gamma=log_gamma,
        mu=row(W['rwkv_mu'][l]), vecs=vecs,
        w2p=pad_rows(W['rwkv_w2'][l], 0), a2p=pad_rows(W['rwkv_a2'][l], C_DECAY_RANK),
        g2p=pad_rows(W['rwkv_g2'][l], C_DECAY_RANK + C_AAA_RANK),
        woa=w_out[:A_W], wob=w_out[A_W:A_W + B_W], woc=w_out[A_W + B_W:],
        ncross=row(W['norm_cross'][l]), wq=W['wq_x'][l].astype(BF16), wo=W['wo_x'][l].astype(BF16),
    )


def _trunk(x, pos, mem_k, mem_v, st_a, st_b, st_c, st_sh, layers, final_g):
    bsz, t, d = x.shape
    n = bsz * t
    depth = len(layers)
    cos_t, sin_t = _rope_tables(pos)
    new_a, new_b, new_c, new_sh = [], [], [], []
    x2 = x.reshape(n, d)
    for l, p in enumerate(layers):
        x2 = _ffn(x2, p['n1'], p['g1'], p['u1'], p['d1'], final_g, False)
        x3, ma, mb, sc, sh = _mixer(
            x2.reshape(bsz, t, d), p, cos_t, sin_t, _pair_pack(st_a[l].transpose(0, 1, 3, 2)),
            _blockdiag_t(st_b[l]), st_sh[l], _pair_pack(st_c[l]),
            mem_k.reshape(depth, bsz, -1, d), mem_v.reshape(depth, bsz, -1, d), l)
        x2 = _ffn(x3.reshape(n, d), p['n2'], p['g2'], p['u2'], p['d2'], final_g, l == depth - 1)
        new_a.append(_pair_unpack(ma).transpose(0, 1, 3, 2))
        new_b.append(_unblockdiag_t(mb, B_HEADS, B_DK, B_DV))
        new_c.append(_pair_unpack(sc))
        new_sh.append(sh)
    return (x2.reshape(bsz, t, d), jnp.stack(new_a), jnp.stack(new_b), jnp.stack(new_c),
            jnp.stack(new_sh))


def kernel(x_prompt, x_sample, mem_prompt, cache_mem_k, cache_mem_v, state_hgrn, state_ret, state_rwkv, state_rwkv_shift, norm_ffn1, ffn1_wg, ffn1_wu, ffn1_wd, norm_mix, w_in, hgrn_lb_param, hgrn_norm, rwkv_mu, rwkv_w0, rwkv_w2, rwkv_a0, rwkv_a2, rwkv_g2, rwkv_k_k, rwkv_k_a, rwkv_r_k, rwkv_ln_w, rwkv_ln_b, w_out, norm_cross, norm_mem, wq_x, wk_x, wv_x, wo_x, norm_ffn2, ffn2_wg, ffn2_wu, ffn2_wd, final_norm):
    W = dict(norm_ffn1=norm_ffn1, ffn1_wg=ffn1_wg, ffn1_wu=ffn1_wu, ffn1_wd=ffn1_wd, norm_mix=norm_mix,
             w_in=w_in, hgrn_norm=hgrn_norm, rwkv_mu=rwkv_mu, rwkv_w0=rwkv_w0, rwkv_w2=rwkv_w2,
             rwkv_a0=rwkv_a0, rwkv_a2=rwkv_a2, rwkv_g2=rwkv_g2, rwkv_k_k=rwkv_k_k, rwkv_k_a=rwkv_k_a,
             rwkv_r_k=rwkv_r_k, rwkv_ln_w=rwkv_ln_w, rwkv_ln_b=rwkv_ln_b, w_out=w_out,
             norm_cross=norm_cross, wq_x=wq_x, wo_x=wo_x, norm_ffn2=norm_ffn2, ffn2_wg=ffn2_wg,
             ffn2_wu=ffn2_wu, ffn2_wd=ffn2_wd)
    depth = w_in.shape[0]
    d = x_prompt.shape[-1]
    sm = jax.nn.softmax(hgrn_lb_param.astype(F32), axis=0)
    lbs = jnp.cumsum(sm, axis=0) - sm[0:1]
    log_gamma = jnp.log1p(-jnp.exp2(-5.0 - jnp.arange(B_HEADS, dtype=F32)))
    layers = [_prep_layer(l, W, lbs, log_gamma) for l in range(depth)]
    final_g = final_norm.reshape(1, -1).astype(F32)

    bp, tp, _ = x_prompt.shape
    n_mem = mem_prompt.shape[1]
    hd = d // MEM_HEADS
    mem2 = mem_prompt.reshape(bp * n_mem, d)
    mks, mvs = [], []
    for l in range(depth):
        mk, mv = _memkv(mem2, norm_mem[l].reshape(1, -1), wk_x[l].astype(BF16), wv_x[l].astype(BF16))
        mks.append(mk.reshape(bp, n_mem, MEM_HEADS, hd))
        mvs.append(mv.reshape(bp, n_mem, MEM_HEADS, hd))
    p_mem_k = jnp.stack(mks)
    p_mem_v = jnp.stack(mvs)

    zeros = lambda *s: jnp.zeros((depth, bp) + s, F32)
    y_prompt, p_hgrn, p_ret, p_rwkv, p_shift = _trunk(
        x_prompt, jnp.arange(tp, dtype=jnp.int32), p_mem_k, p_mem_v,
        zeros(A_HEADS, A_DK, A_DV), zeros(B_HEADS, B_DK, B_DV), zeros(C_HEADS, C_DH, C_DH),
        zeros(1, C_COLS), layers, final_g)

    ts = x_sample.shape[1]
    y_sample, s_hgrn, s_ret, s_rwkv, s_shift = _trunk(
        x_sample, PAST_LEN + jnp.arange(ts, dtype=jnp.int32), cache_mem_k, cache_mem_v,
        state_hgrn, state_ret, state_rwkv, state_rwkv_shift, layers, final_g)

    return (y_prompt, y_sample, p_hgrn, p_ret, p_rwkv, p_shift, p_mem_k, p_mem_v,
            s_hgrn, s_ret, s_rwkv, s_shift)
```

```python
import functools

import jax
import jax.numpy as jnp
from jax import lax
from jax.experimental import pallas as pl
from jax.experimental.pallas import tpu as pltpu

F32 = jnp.float32
BF16 = jnp.bfloat16

EPS = 1e-6
LB_FLOOR = 1e-20
ROPE_BASE = 10000.0
PAST_LEN = 4096
MEM_HEADS = 4
A_HEADS, A_DK, A_DV = 4, 64, 64
A_KW = A_HEADS * A_DK
A_W = A_HEADS * A_DV
A_COLS = 2 * A_KW + 2 * A_W
B_HEADS, B_DK, B_DV = 6, 32, 64
B_KW = B_HEADS * B_DK
B_W = B_HEADS * B_DV
B_COLS = 2 * B_KW + 2 * B_W
C_HEADS, C_DH = 6, 64
C_W = C_HEADS * C_DH
C_DECAY_RANK, C_AAA_RANK, C_GATE_RANK = 32, 32, 64
C_LORA = C_DECAY_RANK + C_AAA_RANK + C_GATE_RANK
C_COLS = 3 * C_W + C_LORA
RWKV_GN_EPS = 64e-5
LOG2E = 1.4426950408889634

LANES = 128
VMEM_LIMIT = 56 * 1024 * 1024
FFN_TILE = 1024
FFN_FCHUNK = 256
TOK_TILE = 512
ATTN_TILE = 1024
SUB_ROWS = 512
MIXER_TILE = 256
HGRN_BLOCK = 16
RWKV_CHUNK = 64
RWKV_GROUP_CHUNKS = 2


def _cparams(*sem):
    return pltpu.CompilerParams(dimension_semantics=sem, vmem_limit_bytes=VMEM_LIMIT)


def _mm(a, b):
    return jnp.dot(a.astype(BF16), b.astype(BF16), preferred_element_type=F32)


def _mm_nt(a, b):
    return lax.dot_general(a.astype(BF16), b.astype(BF16), (((1,), (1,)), ((), ())),
                           preferred_element_type=F32)


def _mm_tn(a, b):
    return lax.dot_general(a.astype(BF16), b.astype(BF16), (((0,), (0,)), ((), ())),
                           preferred_element_type=F32)


def _split3(x):
    hi = x.astype(BF16)
    r = x - hi.astype(F32)
    mid = r.astype(BF16)
    lo = (r - mid.astype(F32)).astype(BF16)
    return hi, mid, lo


def _mm_exact_lhs(c, x):
    hi, mid, lo = _split3(x)
    return (jnp.dot(c, hi, preferred_element_type=F32) + jnp.dot(c, mid, preferred_element_type=F32)
            + jnp.dot(c, lo, preferred_element_type=F32))


def _rms(x, g):
    return x * lax.rsqrt(jnp.mean(x * x, axis=-1, keepdims=True) + EPS) * g


def _sigmoid(x):
    return 0.5 * jnp.tanh(0.5 * x) + 0.5


def _silu(x):
    return x * _sigmoid(x)


def _softplus(x):
    return jnp.maximum(x, 0.0) + jnp.log(1.0 + jnp.exp(-jnp.abs(x)))


def _row_subtiles(rows):
    n = max(1, rows // SUB_ROWS)
    step = rows // n
    return [slice(i * step, (i + 1) * step) for i in range(n)]


def _iota2(shape, axis):
    return lax.broadcasted_iota(jnp.int32, shape, axis)


def _block_ones(n, blk):
    return (_iota2((n, n), 0) // blk) == (_iota2((n, n), 1) // blk)


def _seg_sum(x, blk):
    n = x.shape[-1]
    outs = []
    for lo in range(0, n, LANES):
        w = min(LANES, n - lo)
        outs.append(jnp.dot(x[:, lo:lo + w].astype(BF16), _block_ones(w, blk).astype(BF16),
                            preferred_element_type=F32))
    return outs[0] if len(outs) == 1 else jnp.concatenate(outs, axis=-1)


def _ffn_body(x_ref, g_ref, wg_ref, wu_ref, wd_ref, gf_ref, o_ref, *, apply_final_norm):
    x = x_ref[...]
    hb = _rms(x, g_ref[...]).astype(BF16)
    d_ff = wg_ref.shape[1]
    acc = jnp.zeros(x.shape, F32)
    for j in range(d_ff // FFN_FCHUNK):
        cols = slice(j * FFN_FCHUNK, (j + 1) * FFN_FCHUNK)
        a = jnp.dot(hb, wg_ref[:, cols], preferred_element_type=F32)
        u = jnp.dot(hb, wu_ref[:, cols], preferred_element_type=F32)
        t = (_silu(a) * u).astype(BF16)
        acc = acc + jnp.dot(t, wd_ref[cols, :], preferred_element_type=F32)
    y = x + 0.5 * acc
    if apply_final_norm:
        y = _rms(y, gf_ref[...])
    o_ref[...] = y


def _ffn(x, g, wg, wu, wd, gf, apply_final_norm):
    n, d = x.shape
    d_ff = wg.shape[1]
    tm = min(FFN_TILE, n)
    const = lambda i: (0, 0)
    resident = lambda shape: pl.BlockSpec(shape, const, pipeline_mode=pl.Buffered(1))
    return pl.pallas_call(
        functools.partial(_ffn_body, apply_final_norm=apply_final_norm),
        grid=(n // tm,),
        in_specs=[pl.BlockSpec((tm, d), lambda i: (i, 0)),
                  pl.BlockSpec((1, d), const),
                  resident((d, d_ff)),
                  resident((d, d_ff)),
                  resident((d_ff, d)),
                  pl.BlockSpec((1, d), const)],
        out_specs=pl.BlockSpec((tm, d), lambda i: (i, 0)),
        out_shape=jax.ShapeDtypeStruct((n, d), F32),
        compiler_params=_cparams("parallel"),
        name="ffn",
    )(x, g, wg, wu, wd, gf)


def _memkv_body(m_ref, g_ref, wk_ref, wv_ref, k_ref, v_ref):
    hb = _rms(m_ref[...], g_ref[...]).astype(BF16)
    k_ref[...] = jnp.dot(hb, wk_ref[...], preferred_element_type=F32)
    v_ref[...] = jnp.dot(hb, wv_ref[...], preferred_element_type=F32)


def _memkv(mem, g, wk, wv):
    n, d = mem.shape
    tm = min(TOK_TILE, n)
    const = lambda i: (0, 0)
    row = lambda i: (i, 0)
    return pl.pallas_call(
        _memkv_body,
        grid=(n // tm,),
        in_specs=[pl.BlockSpec((tm, d), row), pl.BlockSpec((1, d), const),
                  pl.BlockSpec((d, d), const), pl.BlockSpec((d, d), const)],
        out_specs=[pl.BlockSpec((tm, d), row), pl.BlockSpec((tm, d), row)],
        out_shape=[jax.ShapeDtypeStruct((n, d), F32)] * 2,
        compiler_params=_cparams("parallel"),
        name="memkv",
    )(mem, g, wk, wv)


def _hgrn_tile(x, lbf_ref, oml_ref, gn_ref, m_scr):
    tile = x.shape[0]
    blk = HGRN_BLOCK
    nb = tile // blk
    q = x[:, 0:A_KW]
    f = x[:, A_KW:2 * A_KW]
    v = x[:, 2 * A_KW:2 * A_KW + A_W]
    g = x[:, 2 * A_KW + A_W:A_COLS]
    gate_f = lbf_ref[...] + oml_ref[...] * _sigmoid(f)
    logf = jnp.log(gate_f)
    ka = 1.0 - gate_f
    qa = _silu(q) * (A_DK ** -0.5)
    ti = _iota2((tile, tile), 0)
    si = _iota2((tile, tile), 1)
    same_blk = (ti // blk) == (si // blk)
    b = _mm_exact_lhs((same_blk & (ti >= si)).astype(BF16), logf)
    b_tot = jnp.broadcast_to(b.reshape(nb, blk, A_KW)[:, blk - 1:blk, :],
                             (nb, blk, A_KW)).reshape(tile, A_KW)
    q_t = qa * jnp.exp(b)
    k_hat = ka * jnp.exp(b_tot - b)
    d_blk = jnp.exp(b_tot)
    head_ones = _block_ones(A_KW, A_DK)
    head_ones_b = head_ones.astype(BF16)

    hb = blk // 2
    halves = lambda a: (lambda a4: (a4[:, 0], a4[:, 1]))(a.reshape(nb, 2, hb, A_KW))
    b_h, ka_h, qa_h, v_h = halves(b * LOG2E), halves(ka), halves(qa), halves(v)
    src_row = lambda a_h, s: a_h[s // hb][:, s % hb:s % hb + 1, :]
    row_h = lax.broadcasted_iota(jnp.int32, (nb, hb, A_KW), 1)
    slab_keys, slabs = [], []
    for s in range(blk):
        for half in range(s // hb, 2):
            z = qa_h[half] * src_row(ka_h, s) * jnp.exp2(jnp.minimum(b_h[half] - src_row(b_h, s), 0.0))
            if half == s // hb:
                z = jnp.where(row_h >= s % hb, z, 0.0)
            slab_keys.append((s, half))
            slabs.append(z.reshape(nb * hb, A_KW))
    att = _mm(jnp.concatenate(slabs, axis=0), head_ones_b)
    o_h = [jnp.zeros((nb, hb, A_W), F32), jnp.zeros((nb, hb, A_W), F32)]
    for i, (s, half) in enumerate(slab_keys):
        part = att[i * nb * hb:(i + 1) * nb * hb, :].reshape(nb, hb, A_W)
        o_h[half] = o_h[half] + part * src_row(v_h, s)
    o3 = jnp.stack(o_h, axis=1)

    npair = A_HEADS // 2
    pair_diag = _block_ones(LANES, A_DK)
    part = lambda a, j, p: a[j * blk:(j + 1) * blk, p * LANES:(p + 1) * LANES]
    outer = [[jnp.where(pair_diag, _mm_tn(part(v, j, p), part(k_hat, j, p)), 0.0) for p in range(npair)]
             for j in range(nb)]
    m = [m_scr[p] for p in range(npair)]
    states = []
    for j in range(nb):
        states.append(m)
        m = [m[p] * d_blk[j * blk:j * blk + 1, p * LANES:(p + 1) * LANES] + outer[j][p] for p in range(npair)]
    for p in range(npair):
        m_scr[p] = m[p]
    o = jnp.concatenate(
        [jnp.concatenate([_mm_nt(part(q_t, j, p), states[j][p]) for p in range(npair)], axis=1)
         for j in range(nb)], axis=0)
    o = o + o3.reshape(tile, A_W)
    ms = _mm(o * o, head_ones_b) * (1.0 / A_DV)
    return o * lax.rsqrt(ms + EPS) * gn_ref[...] * _silu(g)


def _ret_tables(lgs_ref, lg_ref, d_scr, eq_scr, ev_scr):
    tile = eq_scr.shape[0]
    rel = (_iota2((tile, tile), 0) - _iota2((tile, tile), 1)).astype(F32)
    for h in range(B_HEADS):
        d_scr[h] = jnp.where(rel >= 0.0, jnp.exp(lgs_ref[h] * jnp.maximum(rel, 0.0)), 0.0)
    j_k = _iota2((tile, B_KW), 0).astype(F32)
    j_v = _iota2((tile, B_W), 0).astype(F32)
    eq_scr[...] = jnp.exp(lg_ref[0:1, 0:B_KW] * (j_k + 1.0))
    ev_scr[...] = jnp.exp(lg_ref[1:2, :] * (tile - 1.0 - j_v))


def _ret_tile(x, cos_ref, sin_ref, lg_ref, m_scr, d_scr, eq_scr, ev_scr):
    tile = x.shape[0]
    qk = x[:, 0:2 * B_KW]
    v = x[:, 2 * B_KW:2 * B_KW + B_W]
    g = x[:, 2 * B_KW + B_W:B_COLS]
    half = B_DK // 2
    first_half = (_iota2(qk.shape, 1) % B_DK) < half
    swapped = jnp.where(first_half, -pltpu.roll(qk, 2 * B_KW - half, 1), pltpu.roll(qk, half, 1))
    qk = qk * cos_ref[...] + swapped * sin_ref[...]
    q = qk[:, 0:B_KW]
    k = qk[:, B_KW:2 * B_KW] * (B_DK ** -0.5)

    lg_k = lg_ref[0:1, 0:B_KW]
    m = m_scr[...]
    cross = _mm_nt(q * eq_scr[...], m)
    scores = [_mm_nt(q[:, h * B_DK:(h + 1) * B_DK], k[:, h * B_DK:(h + 1) * B_DK]) for h in range(B_HEADS)]
    outs = [_mm(scores[h] * d_scr[h], v[:, h * B_DV:(h + 1) * B_DV]) for h in range(B_HEADS)]
    o = jnp.concatenate(outs, axis=-1) + cross
    mu = _seg_sum(o, B_DV) * (1.0 / B_DV)
    oc = o - mu
    var = _seg_sum(oc * oc, B_DV) * (1.0 / B_DV)
    v_w = v * ev_scr[...]
    head_mask = (_iota2((B_W, B_KW), 0) // B_DV) == (_iota2((B_W, B_KW), 1) // B_DK)
    m_scr[...] = m * jnp.exp(lg_k * float(tile)) + jnp.where(head_mask, _mm_tn(v_w, k), 0.0)
    return oc * lax.rsqrt(var + EPS) * _silu(g)


def _rwkv_tile(p, mu_ref, vec_ref, w2_ref, a2_ref, g2_ref, s_scr, sh_scr):
    L = RWKV_CHUNK
    tl = p.shape[0]
    nchunk = tl // L
    npair = C_HEADS // 2
    row = _iota2((tl, C_COLS), 0)
    prev = jnp.where(row == 0, sh_scr[...], pltpu.roll(p, 1, 0))
    xs = p + mu_ref[...] * (prev - p)
    sh_scr[...] = p[tl - 1:tl, :]

    r = xs[:, 0:C_W]
    k = xs[:, C_W:2 * C_W]
    v = xs[:, 2 * C_W:3 * C_W]
    lora = xs[:, 3 * C_W:C_COLS]
    w0, a0, k_k, k_a, r_k, ln_w, ln_b = (vec_ref[i:i + 1, :] for i in range(7))
    w = -_softplus(-(w0 + _mm(jnp.tanh(lora), w2_ref[...]))) - 0.5
    lw = -jnp.exp(w)
    a = _sigmoid(a0 + _mm(lora, a2_ref[...]))
    gate = _mm(_sigmoid(lora), g2_ref[...])
    kk = k * k_k
    kk = kk * lax.rsqrt(jnp.maximum(_seg_sum(kk * kk, C_DH), 1e-12))
    kc = k * (1.0 + (a - 1.0) * k_a)
    beta = kk * a

    ti = _iota2((tl, tl), 0)
    si = _iota2((tl, tl), 1)
    same_chunk = (ti // L) == (si // L)
    b = _mm_exact_lhs((same_chunk & (ti >= si)).astype(BF16), lw)
    b_tot = jnp.broadcast_to(b.reshape(nchunk, L, C_W)[:, L - 1:L, :],
                             (nchunk, L, C_W)).reshape(tl, C_W)
    e_nb = jnp.exp(-b)
    e_tail = jnp.exp(b_tot - b)
    kap_t = kk * jnp.exp(b - lw)
    r_t = r * jnp.exp(b)
    beta_t = beta * e_nb
    k_t = kc * e_nb
    beta_h = beta * e_tail
    k_h = kc * e_tail
    p_last = jnp.exp(b_tot)

    gch = min(RWKV_GROUP_CHUNKS, nchunk)
    nblk = 2 * gch
    wcat = L * nblk
    t_cat = _iota2((L, wcat), 0)
    s_cat = _iota2((L, wcat), 1) % L
    blk_cat = _iota2((L, wcat), 1) // L
    strict_cat = t_cat > s_cat
    incl_cat = t_cat >= s_cat
    eye_cat = (t_cat == s_cat).astype(F32)
    head0 = (_iota2((L, LANES), 1) // C_DH) == 0
    hmask = [head0, jnp.logical_not(head0)]
    pair_diag = _block_ones(LANES, C_DH)
    zeros_p = jnp.zeros((L, LANES), BF16)

    cut = lambda x, ci, pr: x[ci * L:(ci + 1) * L, pr * LANES:(pr + 1) * LANES]
    items = [(ci, pr) for ci in range(nchunk) for pr in range(npair)]
    groups = [(pr, cg) for cg in range(nchunk // gch) for pr in range(npair)]
    members = lambda cg: range(cg * gch, (cg + 1) * gch)

    def block_diag(x):
        xb = x.astype(BF16)
        return jnp.concatenate([jnp.where(blk_cat == j, xb, jnp.zeros_like(xb)) for j in range(nblk)], axis=0)

    def placed(ci, parts):
        width = len(parts) * LANES
        body = parts[0] if len(parts) == 1 else jnp.concatenate(parts, axis=1)
        secs = [body if cj == ci % gch else jnp.zeros((L, width), BF16) for cj in range(gch)]
        return secs[0] if gch == 1 else jnp.concatenate(secs, axis=1)

    def head_part(x, hh):
        return jnp.where(hmask[hh], x.astype(BF16), zeros_p)

    a_cat, b_cat, c_cat, e_cat = {}, {}, {}, {}
    gram = {}
    for ci, pr in items:
        lhs = jnp.concatenate([jnp.where(hmask[hh], cut(x, ci, pr), 0.0)
                               for hh in range(2) for x in (kap_t, r_t)], axis=0)
        both = _mm_nt(lhs, jnp.concatenate([cut(beta_t, ci, pr), cut(k_t, ci, pr)], axis=0))
        gram[ci, pr, 0] = both[0:2 * L]
        gram[ci, pr, 1] = pltpu.roll(both[2 * L:4 * L], LANES // 2, 1)
    top = lambda g0, g1: jnp.where(head0, g0[0:L], g1[0:L])
    bot = lambda g0, g1: jnp.where(head0, g0[L:2 * L], g1[L:2 * L])
    join = lambda xs: xs[0] if gch == 1 else jnp.concatenate(xs, axis=1)
    for g in groups:
        pr, cg = g
        a_cat[g] = jnp.where(strict_cat, join([top(gram[ci, pr, 0], gram[ci, pr, 1]) for ci in members(cg)]), 0.0)
        b_cat[g] = jnp.where(strict_cat, join([top(gram[ci, pr, 1], gram[ci, pr, 0]) for ci in members(cg)]), 0.0)
        c_cat[g] = jnp.where(incl_cat, join([bot(gram[ci, pr, 0], gram[ci, pr, 1]) for ci in members(cg)]), 0.0)
        e_cat[g] = jnp.where(incl_cat, join([bot(gram[ci, pr, 1], gram[ci, pr, 0]) for ci in members(cg)]), 0.0)
    order_ac = (0, 1)
    order_be = (1, 0)

    n_pow = {g: -a_cat[g] for g in groups}
    t_inv = {g: eye_cat + n_pow[g] for g in groups}
    n_pow = {g: jnp.dot(n_pow[g].astype(BF16), block_diag(n_pow[g]), preferred_element_type=F32) for g in groups}
    for _ in range(L.bit_length() - 3):
        both = {g: jnp.dot(jnp.concatenate([n_pow[g], t_inv[g]], axis=0).astype(BF16), block_diag(n_pow[g]),
                           preferred_element_type=F32) for g in groups}
        n_pow = {g: both[g][0:L] for g in groups}
        t_inv = {g: t_inv[g] + both[g][L:2 * L] for g in groups}
    t_inv = {g: t_inv[g] + jnp.dot(t_inv[g].astype(BF16), block_diag(n_pow[g]), preferred_element_type=F32)
             for g in groups}

    def stacked(cg, order, part_fn):
        return jnp.concatenate([placed(ci, part_fn(ci, hh)) for ci in members(cg) for hh in order], axis=0)

    sec = lambda x, ci, width: x[:, (ci % gch) * width:(ci % gch + 1) * width]
    bv = {(pr, cg): jnp.dot(b_cat[pr, cg].astype(BF16),
                            stacked(cg, order_be, lambda ci, hh: [head_part(cut(v, ci, pr), hh)]),
                            preferred_element_type=F32) for pr, cg in groups}
    rows2 = lambda order, part_fn: jnp.concatenate(
        [jnp.concatenate(part_fn(hh), axis=1) for hh in order], axis=0)
    tr = {(ci, pr): jnp.dot(sec(t_inv[pr, ci // gch], ci, LANES).astype(BF16),
                            rows2(order_ac, lambda hh: [head_part(sec(bv[pr, ci // gch], ci, LANES), hh),
                                                        head_part(cut(kap_t, ci, pr), hh)]),
                            preferred_element_type=F32) for ci, pr in items}
    u0 = {it: -tr[it][:, 0:LANES] for it in items}
    wm = {it: tr[it][:, LANES:2 * LANES] for it in items}
    ce = {(ci, pr): jnp.dot(
        jnp.concatenate([sec(c_cat[pr, ci // gch], ci, LANES), sec(e_cat[pr, ci // gch], ci, LANES)],
                        axis=1).astype(BF16),
        jnp.concatenate([rows2(order_ac, lambda hh: [head_part(u0[ci, pr], hh), head_part(wm[ci, pr], hh)]),
                         rows2(order_be, lambda hh: [head_part(cut(v, ci, pr), hh), zeros_p])], axis=0),
        preferred_element_type=F32) for ci, pr in items}
    y0 = {it: ce[it][:, 0:LANES] for it in items}
    r_hat = {(ci, pr): cut(r_t, ci, pr) - ce[ci, pr][:, LANES:2 * LANES] for ci, pr in items}
    q_m = {it: -jnp.where(pair_diag, _mm_tn(wm[it], cut(beta_h, *it)), 0.0) for it in items}
    z_m = {it: jnp.where(pair_diag,
                         _mm_tn(jnp.concatenate([u0[it], cut(v, *it)], axis=0),
                                jnp.concatenate([cut(beta_h, *it), cut(k_h, *it)], axis=0)), 0.0)
           for it in items}

    s_mat = [s_scr[pr] for pr in range(npair)]
    y_rows = []
    for ci in range(nchunk):
        y_dep = [_mm_nt(r_hat[ci, pr], s_mat[pr]) for pr in range(npair)]
        s_q = [_mm(s_mat[pr], q_m[ci, pr]) for pr in range(npair)]
        s_mat = [s_mat[pr] * p_last[ci * L:ci * L + 1, pr * LANES:(pr + 1) * LANES] + s_q[pr] + z_m[ci, pr]
                 for pr in range(npair)]
        y_rows.append(jnp.concatenate([y_dep[pr] + y0[ci, pr] for pr in range(npair)], axis=-1))
    for pr in range(npair):
        s_scr[pr] = s_mat[pr]

    y = y_rows[0] if nchunk == 1 else jnp.concatenate(y_rows, axis=0)
    mu_y = _seg_sum(y, C_DH) * (1.0 / C_DH)
    yc = y - mu_y
    var = _seg_sum(yc * yc, C_DH) * (1.0 / C_DH)
    yn = yc * lax.rsqrt(var + RWKV_GN_EPS) * ln_w + ln_b
    bonus = _seg_sum(r * kc * r_k, C_DH) * v
    return (yn + bonus) * gate


def _mixer_body(lgs_ref, x_ref, gmix_ref, wa_ref, wb_ref, wc_ref,
                lbf_ref, oml_ref, gn_ref, ma0_ref,
                cos_ref, sin_ref, lg_ref, mb0_ref,
                sh0_ref, s0_ref, mu_ref, vec_ref, w2_ref, a2_ref, g2_ref,
                oa_ref, ob_ref, yc_ref, maout_ref, mbout_ref, sout_ref, shout_ref,
                ma_scr, mb_scr, d_scr, eq_scr, ev_scr, s_scr, sh_scr):
    c = pl.program_id(1)

    @pl.when(c == 0)
    def _():
        ma_scr[...] = ma0_ref[0]
        mb_scr[...] = mb0_ref[0]
        s_scr[...] = s0_ref[0]
        sh_scr[...] = sh0_ref[0]
        _ret_tables(lgs_ref, lg_ref, d_scr, eq_scr, ev_scr)

    hb = _rms(x_ref[0], gmix_ref[...]).astype(BF16)
    pa = jnp.dot(hb, wa_ref[...], preferred_element_type=F32)
    pb = jnp.dot(hb, wb_ref[...], preferred_element_type=F32)
    pc = jnp.dot(hb, wc_ref[...], preferred_element_type=F32)
    oa_ref[0] = _hgrn_tile(pa, lbf_ref, oml_ref, gn_ref, ma_scr)
    ob_ref[0] = _ret_tile(pb, cos_ref, sin_ref, lg_ref, mb_scr, d_scr, eq_scr, ev_scr)
    yc_ref[0] = _rwkv_tile(pc, mu_ref, vec_ref, w2_ref, a2_ref, g2_ref, s_scr, sh_scr)

    @pl.when(c == pl.num_programs(1) - 1)
    def _():
        maout_ref[0] = ma_scr[...]
        mbout_ref[0] = mb_scr[...]
        sout_ref[0] = s_scr[...]
        shout_ref[0] = sh_scr[...]


def _mixer(x, p, cos_t, sin_t, ma0, mb0, sh0, s0):
    bsz, t, d = x.shape
    tile = min(MIXER_TILE, t)
    npa, npc = A_HEADS // 2, C_HEADS // 2
    const = lambda b, c: (0, 0)
    tok = lambda w: pl.BlockSpec((1, tile, w), lambda b, c: (b, c, 0))
    full = lambda a: pl.BlockSpec(a.shape, const, pipeline_mode=pl.Buffered(1))
    pos = lambda w: pl.BlockSpec((tile, w), lambda b, c: (c, 0))
    per_batch = lambda *s: pl.BlockSpec((1,) + s, lambda b, c: (b,) + (0,) * len(s))
    ins = [(p['log_gamma'], pl.BlockSpec(memory_space=pltpu.SMEM)),
           (x, tok(d)), (p['nmix'], full(p['nmix'])),
           (p['wa'], full(p['wa'])), (p['wb'], full(p['wb'])), (p['wc'], full(p['wc'])),
           (p['lb_floor'], full(p['lb_floor'])), (p['one_minus_lb'], full(p['one_minus_lb'])),
           (p['gn'], full(p['gn'])), (ma0, per_batch(npa, LANES, LANES)),
           (cos_t, pos(2 * B_KW)), (sin_t, pos(2 * B_KW)), (p['lg_rows'], full(p['lg_rows'])),
           (mb0, per_batch(B_W, B_KW)),
           (sh0, per_batch(1, C_COLS)), (s0, per_batch(npc, LANES, LANES)),
           (p['mu'], full(p['mu'])), (p['vecs'], full(p['vecs'])),
           (p['w2p'], full(p['w2p'])), (p['a2p'], full(p['a2p'])), (p['g2p'], full(p['g2p']))]
    outs = [((bsz, t, A_W), tok(A_W)), ((bsz, t, B_W), tok(B_W)), ((bsz, t, C_W), tok(C_W)),
            ((bsz, npa, LANES, LANES), per_batch(npa, LANES, LANES)),
            ((bsz, B_W, B_KW), per_batch(B_W, B_KW)),
            ((bsz, npc, LANES, LANES), per_batch(npc, LANES, LANES)),
            ((bsz, 1, C_COLS), per_batch(1, C_COLS))]
    return pl.pallas_call(
        _mixer_body,
        grid=(bsz, t // tile),
        in_specs=[s for _, s in ins],
        out_specs=[s for _, s in outs],
        out_shape=[jax.ShapeDtypeStruct(shape, F32) for shape, _ in outs],
        scratch_shapes=[pltpu.VMEM((npa, LANES, LANES), F32), pltpu.VMEM((B_W, B_KW), F32),
                        pltpu.VMEM((B_HEADS, tile, tile), F32), pltpu.VMEM((tile, B_KW), F32),
                        pltpu.VMEM((tile, B_W), F32),
                        pltpu.VMEM((npc, LANES, LANES), F32), pltpu.VMEM((1, C_COLS), F32)],
        compiler_params=_cparams("parallel", "arbitrary"),
        name="mixer",
    )(*[a for a, _ in ins])


def _outattn_body(x_ref, oa_ref, ob_ref, yc_ref, woa_ref, wob_ref, woc_ref, g_ref, wq_ref, wo_ref,
                  mk_ref, mv_ref, o_ref):
    tm, d = x_ref.shape[1], x_ref.shape[2]
    hd = d // MEM_HEADS
    subs = _row_subtiles(tm)
    heads = [slice(h * hd, (h + 1) * hd) for h in range(MEM_HEADS)]
    mk = mk_ref[0].astype(BF16)
    mv = mv_ref[0].astype(BF16)
    x = [x_ref[0, r, :] + _mm(oa_ref[0, r, :], woa_ref[...]) + _mm(ob_ref[0, r, :], wob_ref[...])
         + _mm(yc_ref[0, r, :], woc_ref[...]) for r in subs]
    q = [_mm(_rms(xi, g_ref[...]), wq_ref[...]) for xi in x]
    s = [[_mm_nt(qi[:, c], mk[:, c]) * (hd ** -0.5) for c in heads] for qi in q]
    o = []
    for si in s:
        e = [jnp.exp(sh - jnp.max(sh, axis=-1, keepdims=True)) for sh in si]
        pr = [eh * (1.0 / jnp.sum(eh, axis=-1, keepdims=True)) for eh in e]
        o.append([_mm(ph, mv[:, c]) for ph, c in zip(pr, heads)])
    for r, xi, oi in zip(subs, x, o):
        o_ref[0, r, :] = xi + _mm(jnp.concatenate(oi, axis=-1), wo_ref[...])


def _outattn(x, oa, ob, yc, woa, wob, woc, g, wq, wo, mk, mv, layer):
    bsz, t, d = x.shape
    tm = min(ATTN_TILE, t)
    n_mem = mk.shape[2]
    const = lambda b, i: (0, 0)
    tok = lambda w: pl.BlockSpec((1, tm, w), lambda b, i: (b, i, 0))
    full = lambda a: pl.BlockSpec(a.shape, const, pipeline_mode=pl.Buffered(1))
    mem = pl.BlockSpec((None, 1, n_mem, d), lambda b, i: (layer, b, 0, 0))
    return pl.pallas_call(
        _outattn_body,
        grid=(bsz, t // tm),
        in_specs=[tok(d), tok(A_W), tok(B_W), tok(C_W), full(woa), full(wob), full(woc), full(g),
                  full(wq), full(wo), mem, mem],
        out_specs=tok(d),
        out_shape=jax.ShapeDtypeStruct((bsz, t, d), F32),
        compiler_params=_cparams("parallel", "parallel"),
        name="outattn",
    )(x, oa, ob, yc, woa, wob, woc, g, wq, wo, mk, mv)


def _blockdiag_t(s):
    bsz, h, dk, dv = s.shape
    eye = jnp.eye(h, dtype=s.dtype)
    return jnp.einsum('bhkv,hg->bhvgk', s, eye).reshape(bsz, h * dv, h * dk)


def _unblockdiag_t(m, h, dk, dv):
    bsz = m.shape[0]
    m5 = m.reshape(bsz, h, dv, h, dk)
    return jnp.stack([m5[:, i, :, i, :] for i in range(h)], axis=1).transpose(0, 1, 3, 2)


def _pair_pack(s):
    bsz, h, d, _ = s.shape
    s4 = s.reshape(bsz, h // 2, 2, d, d)
    eye = jnp.eye(2, dtype=s.dtype)
    return jnp.einsum('bphvk,hg->bphvgk', s4, eye).reshape(bsz, h // 2, 2 * d, 2 * d)


def _pair_unpack(m):
    bsz, npair, w, _ = m.shape
    d = w // 2
    m6 = m.reshape(bsz, npair, 2, d, 2, d)
    return jnp.stack([m6[:, :, i, :, i, :] for i in range(2)], axis=2).reshape(bsz, 2 * npair, d, d)


def _rope_tables(pos):
    half = B_DK // 2
    inv = ROPE_BASE ** (-jnp.arange(half, dtype=F32) / half)
    ang = pos.astype(F32)[:, None] * inv[None, :]
    cos = jnp.tile(jnp.cos(ang), (1, 2 * 2 * B_HEADS))
    sin = jnp.tile(jnp.sin(ang), (1, 2 * 2 * B_HEADS))
    return cos, sin


def _prep_layer(l, W, lbs, log_gamma):
    row = lambda a: a.reshape(1, -1).astype(F32)
    w_in = W['w_in'][l]
    pad_rows = lambda m, off: jnp.zeros((C_LORA, C_W), F32).at[off:off + m.shape[0]].set(m).astype(BF16)
    lb = lbs[l]
    zero = jnp.zeros((C_W,), F32)
    vecs = jnp.stack([W['rwkv_w0'][l], W['rwkv_a0'][l], W['rwkv_k_k'][l], W['rwkv_k_a'][l],
                      W['rwkv_r_k'][l].reshape(-1), W['rwkv_ln_w'][l], W['rwkv_ln_b'][l], zero])
    lg_rows = jnp.stack([jnp.pad(jnp.repeat(log_gamma, B_DK), (0, B_W - B_KW)),
                         jnp.repeat(log_gamma, B_DV)])
    w_out = W['w_out'][l].astype(BF16)
    return dict(
        n1=row(W['norm_ffn1'][l]), g1=W['ffn1_wg'][l].astype(BF16), u1=W['ffn1_wu'][l].astype(BF16),
        d1=W['ffn1_wd'][l].astype(BF16),
        n2=row(W['norm_ffn2'][l]), g2=W['ffn2_wg'][l].astype(BF16), u2=W['ffn2_wu'][l].astype(BF16),
        d2=W['ffn2_wd'][l].astype(BF16),
        nmix=row(W['norm_mix'][l]),
        wa=w_in[:, :A_COLS].astype(BF16), wb=w_in[:, A_COLS:A_COLS + B_COLS].astype(BF16),
        wc=w_in[:, A_COLS + B_COLS:].astype(BF16),
        lb_floor=row(jnp.maximum(lb, LB_FLOOR)), one_minus_lb=row(1.0 - lb),
        gn=row(jnp.tile(W['hgrn_norm'][l], A_HEADS)),
        lg_rows=lg_rows, log_gamma=log_gamma,
        mu=row(W['rwkv_mu'][l]), vecs=vecs,
        w2p=pad_rows(W['rwkv_w2'][l], 0), a2p=pad_rows(W['rwkv_a2'][l], C_DECAY_RANK),
        g2p=pad_rows(W['rwkv_g2'][l], C_DECAY_RANK + C_AAA_RANK),
        woa=w_out[:A_W], wob=w_out[A_W:A_W + B_W], woc=w_out[A_W + B_W:],
        ncross=row(W['norm_cross'][l]), wq=W['wq_x'][l].astype(BF16), wo=W['wo_x'][l].astype(BF16),
    )


def _trunk(x, pos, mem_k, mem_v, st_a, st_b, st_c, st_sh, layers, final_g):
    bsz, t, d = x.shape
    n = bsz * t
    depth = len(layers)
    cos_t, sin_t = _rope_tables(pos)
    new_a, new_b, new_c, new_sh = [], [], [], []
    x2 = x.reshape(n, d)
    for l, p in enumerate(layers):
        x2 = _ffn(x2, p['n1'], p['g1'], p['u1'], p['d1'], final_g, False)
        oa, ob, yc, ma, mb, sc, sh = _mixer(
            x2.reshape(bsz, t, d), p, cos_t, sin_t, _pair_pack(st_a[l].transpose(0, 1, 3, 2)),
            _blockdiag_t(st_b[l]), st_sh[l], _pair_pack(st_c[l]))
        x3 = _outattn(x2.reshape(bsz, t, d), oa, ob, yc, p['woa'], p['wob'], p['woc'], p['ncross'],
                      p['wq'], p['wo'], mem_k.reshape(depth, bsz, -1, d), mem_v.reshape(depth, bsz, -1, d), l)
        x2 = _ffn(x3.reshape(n, d), p['n2'], p['g2'], p['u2'], p['d2'], final_g, l == depth - 1)
        new_a.append(_pair_unpack(ma).transpose(0, 1, 3, 2))
        new_b.append(_unblockdiag_t(mb, B_HEADS, B_DK, B_DV))
        new_c.append(_pair_unpack(sc))
        new_sh.append(sh)
    return (x2.reshape(bsz, t, d), jnp.stack(new_a), jnp.stack(new_b), jnp.stack(new_c),
            jnp.stack(new_sh))


def kernel(x_prompt, x_sample, mem_prompt, cache_mem_k, cache_mem_v, state_hgrn, state_ret, state_rwkv, state_rwkv_shift, norm_ffn1, ffn1_wg, ffn1_wu, ffn1_wd, norm_mix, w_in, hgrn_lb_param, hgrn_norm, rwkv_mu, rwkv_w0, rwkv_w2, rwkv_a0, rwkv_a2, rwkv_g2, rwkv_k_k, rwkv_k_a, rwkv_r_k, rwkv_ln_w, rwkv_ln_b, w_out, norm_cross, norm_mem, wq_x, wk_x, wv_x, wo_x, norm_ffn2, ffn2_wg, ffn2_wu, ffn2_wd, final_norm):
    W = dict(norm_ffn1=norm_ffn1, ffn1_wg=ffn1_wg, ffn1_wu=ffn1_wu, ffn1_wd=ffn1_wd, norm_mix=norm_mix,
             w_in=w_in, hgrn_norm=hgrn_norm, rwkv_mu=rwkv_mu, rwkv_w0=rwkv_w0, rwkv_w2=rwkv_w2,
             rwkv_a0=rwkv_a0, rwkv_a2=rwkv_a2, rwkv_g2=rwkv_g2, rwkv_k_k=rwkv_k_k, rwkv_k_a=rwkv_k_a,
             rwkv_r_k=rwkv_r_k, rwkv_ln_w=rwkv_ln_w, rwkv_ln_b=rwkv_ln_b, w_out=w_out,
             norm_cross=norm_cross, wq_x=wq_x, wo_x=wo_x, norm_ffn2=norm_ffn2, ffn2_wg=ffn2_wg,
             ffn2_wu=ffn2_wu, ffn2_wd=ffn2_wd)
    depth = w_in.shape[0]
    d = x_prompt.shape[-1]
    sm = jax.nn.softmax(hgrn_lb_param.astype(F32), axis=0)
    lbs = jnp.cumsum(sm, axis=0) - sm[0:1]
    log_gamma = jnp.log1p(-jnp.exp2(-5.0 - jnp.arange(B_HEADS, dtype=F32)))
    layers = [_prep_layer(l, W, lbs, log_gamma) for l in range(depth)]
    final_g = final_norm.reshape(1, -1).astype(F32)

    bp, tp, _ = x_prompt.shape
    n_mem = mem_prompt.shape[1]
    hd = d // MEM_HEADS
    mem2 = mem_prompt.reshape(bp * n_mem, d)
    mks, mvs = [], []
    for l in range(depth):
        mk, mv = _memkv(mem2, norm_mem[l].reshape(1, -1), wk_x[l].astype(BF16), wv_x[l].astype(BF16))
        mks.append(mk.reshape(bp, n_mem, MEM_HEADS, hd))
        mvs.append(mv.reshape(bp, n_mem, MEM_HEADS, hd))
    p_mem_k = jnp.stack(mks)
    p_mem_v = jnp.stack(mvs)

    zeros = lambda *s: jnp.zeros((depth, bp) + s, F32)
    y_prompt, p_hgrn, p_ret, p_rwkv, p_shift = _trunk(
        x_prompt, jnp.arange(tp, dtype=jnp.int32), p_mem_k, p_mem_v,
        zeros(A_HEADS, A_DK, A_DV), zeros(B_HEADS, B_DK, B_DV), zeros(C_HEADS, C_DH, C_DH),
        zeros(1, C_COLS), layers, final_g)

    ts = x_sample.shape[1]
    y_sample, s_hgrn, s_ret, s_rwkv, s_shift = _trunk(
        x_sample, PAST_LEN + jnp.arange(ts, dtype=jnp.int32), cache_mem_k, cache_mem_v,
        state_hgrn, state_ret, state_rwkv, state_rwkv_shift, layers, final_g)

    return (y_prompt, y_sample, p_hgrn, p_ret, p_rwkv, p_shift, p_mem_k, p_mem_v,
            s_hgrn, s_ret, s_rwkv, s_shift)
```

```python
import functools

import jax
import jax.numpy as jnp
from jax import lax
from jax.experimental import pallas as pl
from jax.experimental.pallas import tpu as pltpu

F32 = jnp.float32
BF16 = jnp.bfloat16

EPS = 1e-6
LB_FLOOR = 1e-20
ROPE_BASE = 10000.0
PAST_LEN = 4096
MEM_HEADS = 4
A_HEADS, A_DK, A_DV = 4, 64, 64
A_KW = A_HEADS * A_DK
A_W = A_HEADS * A_DV
A_COLS = 2 * A_KW + 2 * A_W
B_HEADS, B_DK, B_DV = 6, 32, 64
B_KW = B_HEADS * B_DK
B_W = B_HEADS * B_DV
B_COLS = 2 * B_KW + 2 * B_W
C_HEADS, C_DH = 6, 64
C_W = C_HEADS * C_DH
C_DECAY_RANK, C_AAA_RANK, C_GATE_RANK = 32, 32, 64
C_LORA = C_DECAY_RANK + C_AAA_RANK + C_GATE_RANK
C_COLS = 3 * C_W + C_LORA
RWKV_GN_EPS = 64e-5
LOG2E = 1.4426950408889634

LANES = 128
VMEM_LIMIT = 56 * 1024 * 1024
FFN_TILE = 1024
FFN_FCHUNK = 256
TOK_TILE = 512
ATTN_TILE = 1024
SUB_ROWS = 512
MIXER_TILE = 256
HGRN_BLOCK = 16
RWKV_CHUNK = 64
RWKV_GROUP_CHUNKS = 2


def _cparams(*sem):
    return pltpu.CompilerParams(dimension_semantics=sem, vmem_limit_bytes=VMEM_LIMIT)


def _mm(a, b):
    return jnp.dot(a.astype(BF16), b.astype(BF16), preferred_element_type=F32)


def _mm_nt(a, b):
    return lax.dot_general(a.astype(BF16), b.astype(BF16), (((1,), (1,)), ((), ())),
                           preferred_element_type=F32)


def _mm_tn(a, b):
    return lax.dot_general(a.astype(BF16), b.astype(BF16), (((0,), (0,)), ((), ())),
                           preferred_element_type=F32)


def _split3(x):
    hi = x.astype(BF16)
    r = x - hi.astype(F32)
    mid = r.astype(BF16)
    lo = (r - mid.astype(F32)).astype(BF16)
    return hi, mid, lo


def _mm_exact_lhs(c, x):
    hi, mid, lo = _split3(x)
    return (jnp.dot(c, hi, preferred_element_type=F32) + jnp.dot(c, mid, preferred_element_type=F32)
            + jnp.dot(c, lo, preferred_element_type=F32))


def _rms(x, g):
    return x * lax.rsqrt(jnp.mean(x * x, axis=-1, keepdims=True) + EPS) * g


def _sigmoid(x):
    return 0.5 * jnp.tanh(0.5 * x) + 0.5


def _silu(x):
    return x * _sigmoid(x)


def _softplus(x):
    return jnp.maximum(x, 0.0) + jnp.log(1.0 + jnp.exp(-jnp.abs(x)))


def _row_subtiles(rows):
    n = max(1, rows // SUB_ROWS)
    step = rows // n
    return [slice(i * step, (i + 1) * step) for i in range(n)]


def _iota2(shape, axis):
    return lax.broadcasted_iota(jnp.int32, shape, axis)


def _block_ones(n, blk):
    return (_iota2((n, n), 0) // blk) == (_iota2((n, n), 1) // blk)


def _seg_sum(x, blk):
    n = x.shape[-1]
    outs = []
    for lo in range(0, n, LANES):
        w = min(LANES, n - lo)
        outs.append(jnp.dot(x[:, lo:lo + w].astype(BF16), _block_ones(w, blk).astype(BF16),
                            preferred_element_type=F32))
    return outs[0] if len(outs) == 1 else jnp.concatenate(outs, axis=-1)


def _ffn_body(x_ref, g_ref, wg_ref, wu_ref, wd_ref, gf_ref, o_ref, *, apply_final_norm):
    x = x_ref[...]
    hb = _rms(x, g_ref[...]).astype(BF16)
    d_ff = wg_ref.shape[1]
    acc = jnp.zeros(x.shape, F32)
    for j in range(d_ff // FFN_FCHUNK):
        cols = slice(j * FFN_FCHUNK, (j + 1) * FFN_FCHUNK)
        a = jnp.dot(hb, wg_ref[:, cols], preferred_element_type=F32)
        u = jnp.dot(hb, wu_ref[:, cols], preferred_element_type=F32)
        t = (_silu(a) * u).astype(BF16)
        acc = acc + jnp.dot(t, wd_ref[cols, :], preferred_element_type=F32)
    y = x + 0.5 * acc
    if apply_final_norm:
        y = _rms(y, gf_ref[...])
    o_ref[...] = y


def _ffn(x, g, wg, wu, wd, gf, apply_final_norm):
    n, d = x.shape
    d_ff = wg.shape[1]
    tm = min(FFN_TILE, n)
    const = lambda i: (0, 0)
    resident = lambda shape: pl.BlockSpec(shape, const, pipeline_mode=pl.Buffered(1))
    return pl.pallas_call(
        functools.partial(_ffn_body, apply_final_norm=apply_final_norm),
        grid=(n // tm,),
        in_specs=[pl.BlockSpec((tm, d), lambda i: (i, 0)),
                  pl.BlockSpec((1, d), const),
                  resident((d, d_ff)),
                  resident((d, d_ff)),
                  resident((d_ff, d)),
                  pl.BlockSpec((1, d), const)],
        out_specs=pl.BlockSpec((tm, d), lambda i: (i, 0)),
        out_shape=jax.ShapeDtypeStruct((n, d), F32),
        compiler_params=_cparams("parallel"),
        name="ffn",
    )(x, g, wg, wu, wd, gf)


def _memkv_body(m_ref, g_ref, wk_ref, wv_ref, k_ref, v_ref):
    hb = _rms(m_ref[...], g_ref[...]).astype(BF16)
    k_ref[...] = jnp.dot(hb, wk_ref[...], preferred_element_type=F32)
    v_ref[...] = jnp.dot(hb, wv_ref[...], preferred_element_type=F32)


def _memkv(mem, g, wk, wv):
    n, d = mem.shape
    tm = min(TOK_TILE, n)
    const = lambda i: (0, 0)
    row = lambda i: (i, 0)
    return pl.pallas_call(
        _memkv_body,
        grid=(n // tm,),
        in_specs=[pl.BlockSpec((tm, d), row), pl.BlockSpec((1, d), const),
                  pl.BlockSpec((d, d), const), pl.BlockSpec((d, d), const)],
        out_specs=[pl.BlockSpec((tm, d), row), pl.BlockSpec((tm, d), row)],
        out_shape=[jax.ShapeDtypeStruct((n, d), F32)] * 2,
        compiler_params=_cparams("parallel"),
        name="memkv",
    )(mem, g, wk, wv)


def _hgrn_tile(x, lbf_ref, oml_ref, gn_ref, m_scr):
    tile = x.shape[0]
    blk = HGRN_BLOCK
    nb = tile // blk
    q = x[:, 0:A_KW]
    f = x[:, A_KW:2 * A_KW]
    v = x[:, 2 * A_KW:2 * A_KW + A_W]
    g = x[:, 2 * A_KW + A_W:A_COLS]
    gate_f = lbf_ref[...] + oml_ref[...] * _sigmoid(f)
    logf = jnp.log(gate_f)
    ka = 1.0 - gate_f
    qa = _silu(q) * (A_DK ** -0.5)
    ti = _iota2((tile, tile), 0)
    si = _iota2((tile, tile), 1)
    same_blk = (ti // blk) == (si // blk)
    b = _mm_exact_lhs((same_blk & (ti >= si)).astype(BF16), logf)
    b_tot = jnp.broadcast_to(b.reshape(nb, blk, A_KW)[:, blk - 1:blk, :],
                             (nb, blk, A_KW)).reshape(tile, A_KW)
    q_t = qa * jnp.exp(b)
    k_hat = ka * jnp.exp(b_tot - b)
    d_blk = jnp.exp(b_tot)
    head_ones = _block_ones(A_KW, A_DK)
    head_ones_b = head_ones.astype(BF16)

    hb = blk // 2
    halves = lambda a: (lambda a4: (a4[:, 0], a4[:, 1]))(a.reshape(nb, 2, hb, A_KW))
    b_h, ka_h, qa_h, v_h = halves(b * LOG2E), halves(ka), halves(qa), halves(v)
    src_row = lambda a_h, s: a_h[s // hb][:, s % hb:s % hb + 1, :]
    row_h = lax.broadcasted_iota(jnp.int32, (nb, hb, A_KW), 1)
    slab_keys, slabs = [], []
    for s in range(blk):
        for half in range(s // hb, 2):
            z = qa_h[half] * src_row(ka_h, s) * jnp.exp2(jnp.minimum(b_h[half] - src_row(b_h, s), 0.0))
            if half == s // hb:
                z = jnp.where(row_h >= s % hb, z, 0.0)
            slab_keys.append((s, half))
            slabs.append(z.reshape(nb * hb, A_KW))
    att = _mm(jnp.concatenate(slabs, axis=0), head_ones_b)
    o_h = [jnp.zeros((nb, hb, A_W), F32), jnp.zeros((nb, hb, A_W), F32)]
    for i, (s, half) in enumerate(slab_keys):
        part = att[i * nb * hb:(i + 1) * nb * hb, :].reshape(nb, hb, A_W)
        o_h[half] = o_h[half] + part * src_row(v_h, s)
    o3 = jnp.stack(o_h, axis=1)

    npair = A_HEADS // 2
    pair_diag = _block_ones(LANES, A_DK)
    part = lambda a, j, p: a[j * blk:(j + 1) * blk, p * LANES:(p + 1) * LANES]
    outer = [[jnp.where(pair_diag, _mm_tn(part(v, j, p), part(k_hat, j, p)), 0.0) for p in range(npair)]
             for j in range(nb)]
    m = [m_scr[p] for p in range(npair)]
    states = []
    for j in range(nb):
        states.append(m)
        m = [m[p] * d_blk[j * blk:j * blk + 1, p * LANES:(p + 1) * LANES] + outer[j][p] for p in range(npair)]
    for p in range(npair):
        m_scr[p] = m[p]
    o = jnp.concatenate(
        [jnp.concatenate([_mm_nt(part(q_t, j, p), states[j][p]) for p in range(npair)], axis=1)
         for j in range(nb)], axis=0)
    o = o + o3.reshape(tile, A_W)
    ms = _mm(o * o, head_ones_b) * (1.0 / A_DV)
    return o * lax.rsqrt(ms + EPS) * gn_ref[...] * _silu(g)


def _ret_tables(lgs_ref, lg_ref, d_scr, eq_scr, ev_scr):
    tile = eq_scr.shape[0]
    rel = (_iota2((tile, tile), 0) - _iota2((tile, tile), 1)).astype(F32)
    for h in range(B_HEADS):
        d_scr[h] = jnp.where(rel >= 0.0, jnp.exp(lgs_ref[h] * jnp.maximum(rel, 0.0)), 0.0)
    j_k = _iota2((tile, B_KW), 0).astype(F32)
    j_v = _iota2((tile, B_W), 0).astype(F32)
    eq_scr[...] = jnp.exp(lg_ref[0:1, 0:B_KW] * (j_k + 1.0))
    ev_scr[...] = jnp.exp(lg_ref[1:2, :] * (tile - 1.0 - j_v))


def _ret_tile(x, cos_ref, sin_ref, lg_ref, m_scr, d_scr, eq_scr, ev_scr):
    tile = x.shape[0]
    qk = x[:, 0:2 * B_KW]
    v = x[:, 2 * B_KW:2 * B_KW + B_W]
    g = x[:, 2 * B_KW + B_W:B_COLS]
    half = B_DK // 2
    first_half = (_iota2(qk.shape, 1) % B_DK) < half
    swapped = jnp.where(first_half, -pltpu.roll(qk, 2 * B_KW - half, 1), pltpu.roll(qk, half, 1))
    qk = qk * cos_ref[...] + swapped * sin_ref[...]
    q = qk[:, 0:B_KW]
    k = qk[:, B_KW:2 * B_KW] * (B_DK ** -0.5)

    lg_k = lg_ref[0:1, 0:B_KW]
    m = m_scr[...]
    cross = _mm_nt(q * eq_scr[...], m)
    scores = [_mm_nt(q[:, h * B_DK:(h + 1) * B_DK], k[:, h * B_DK:(h + 1) * B_DK]) for h in range(B_HEADS)]
    outs = [_mm(scores[h] * d_scr[h], v[:, h * B_DV:(h + 1) * B_DV]) for h in range(B_HEADS)]
    o = jnp.concatenate(outs, axis=-1) + cross
    mu = _seg_sum(o, B_DV) * (1.0 / B_DV)
    oc = o - mu
    var = _seg_sum(oc * oc, B_DV) * (1.0 / B_DV)
    v_w = v * ev_scr[...]
    head_mask = (_iota2((B_W, B_KW), 0) // B_DV) == (_iota2((B_W, B_KW), 1) // B_DK)
    m_scr[...] = m * jnp.exp(lg_k * float(tile)) + jnp.where(head_mask, _mm_tn(v_w, k), 0.0)
    return oc * lax.rsqrt(var + EPS) * _silu(g)


def _rwkv_tile(p, mu_ref, vec_ref, wl_ref, s_scr, sh_scr):
    L = RWKV_CHUNK
    tl = p.shape[0]
    nchunk = tl // L
    npair = C_HEADS // 2
    row = _iota2((tl, C_COLS), 0)
    prev = jnp.where(row == 0, sh_scr[...], pltpu.roll(p, 1, 0))
    xs = p + mu_ref[...] * (prev - p)
    sh_scr[...] = p[tl - 1:tl, :]

    r = xs[:, 0:C_W]
    k = xs[:, C_W:2 * C_W]
    v = xs[:, 2 * C_W:3 * C_W]
    lora = xs[:, 3 * C_W:C_COLS]
    w0, a0, k_k, k_a, r_k, ln_w, ln_b = (vec_ref[i:i + 1, :] for i in range(7))
    lane_l = _iota2(lora.shape, 1)
    acts = jnp.where(lane_l < C_DECAY_RANK, jnp.tanh(lora),
                     jnp.where(lane_l < C_DECAY_RANK + C_AAA_RANK, lora, _sigmoid(lora)))
    up = _mm(acts, wl_ref[...])
    w = -_softplus(-(w0 + up[:, 0:C_W])) - 0.5
    lw = -jnp.exp(w)
    a = _sigmoid(a0 + up[:, C_W:2 * C_W])
    gate = up[:, 2 * C_W:3 * C_W]
    kk = k * k_k
    kk = kk * lax.rsqrt(jnp.maximum(_seg_sum(kk * kk, C_DH), 1e-12))
    kc = k * (1.0 + (a - 1.0) * k_a)
    beta = kk * a

    ti = _iota2((tl, tl), 0)
    si = _iota2((tl, tl), 1)
    same_chunk = (ti // L) == (si // L)
    b = _mm_exact_lhs((same_chunk & (ti >= si)).astype(BF16), lw)
    b_tot = jnp.broadcast_to(b.reshape(nchunk, L, C_W)[:, L - 1:L, :],
                             (nchunk, L, C_W)).reshape(tl, C_W)
    e_nb = jnp.exp(-b)
    e_tail = jnp.exp(b_tot - b)
    kap_t = kk * jnp.exp(b - lw)
    r_t = r * jnp.exp(b)
    beta_t = beta * e_nb
    k_t = kc * e_nb
    beta_h = beta * e_tail
    k_h = kc * e_tail
    p_last = jnp.exp(b_tot)

    gch = min(RWKV_GROUP_CHUNKS, nchunk)
    nblk = 2 * gch
    wcat = L * nblk
    t_cat = _iota2((L, wcat), 0)
    s_cat = _iota2((L, wcat), 1) % L
    blk_cat = _iota2((L, wcat), 1) // L
    strict_cat = t_cat > s_cat
    incl_cat = t_cat >= s_cat
    eye_cat = (t_cat == s_cat).astype(F32)
    head0 = (_iota2((L, LANES), 1) // C_DH) == 0
    hmask = [head0, jnp.logical_not(head0)]
    pair_diag = _block_ones(LANES, C_DH)
    zeros_p = jnp.zeros((L, LANES), BF16)

    cut = lambda x, ci, pr: x[ci * L:(ci + 1) * L, pr * LANES:(pr + 1) * LANES]
    items = [(ci, pr) for ci in range(nchunk) for pr in range(npair)]
    groups = [(pr, cg) for cg in range(nchunk // gch) for pr in range(npair)]
    members = lambda cg: range(cg * gch, (cg + 1) * gch)

    def block_diag(x):
        xb = x.astype(BF16)
        return jnp.concatenate([jnp.where(blk_cat == j, xb, jnp.zeros_like(xb)) for j in range(nblk)], axis=0)

    def placed(ci, parts):
        width = len(parts) * LANES
        body = parts[0] if len(parts) == 1 else jnp.concatenate(parts, axis=1)
        secs = [body if cj == ci % gch else jnp.zeros((L, width), BF16) for cj in range(gch)]
        return secs[0] if gch == 1 else jnp.concatenate(secs, axis=1)

    def head_part(x, hh):
        return jnp.where(hmask[hh], x.astype(BF16), zeros_p)

    a_cat, b_cat, c_cat, e_cat = {}, {}, {}, {}
    gram = {}
    for ci, pr in items:
        lhs = jnp.concatenate([jnp.where(hmask[hh], cut(x, ci, pr), 0.0)
                               for hh in range(2) for x in (kap_t, r_t)], axis=0)
        both = _mm_nt(lhs, jnp.concatenate([cut(beta_t, ci, pr), cut(k_t, ci, pr)], axis=0))
        gram[ci, pr, 0] = both[0:2 * L]
        gram[ci, pr, 1] = pltpu.roll(both[2 * L:4 * L], LANES // 2, 1)
    top = lambda g0, g1: jnp.where(head0, g0[0:L], g1[0:L])
    bot = lambda g0, g1: jnp.where(head0, g0[L:2 * L], g1[L:2 * L])
    join = lambda xs: xs[0] if gch == 1 else jnp.concatenate(xs, axis=1)
    for g in groups:
        pr, cg = g
        a_cat[g] = jnp.where(strict_cat, join([top(gram[ci, pr, 0], gram[ci, pr, 1]) for ci in members(cg)]), 0.0)
        b_cat[g] = jnp.where(strict_cat, join([top(gram[ci, pr, 1], gram[ci, pr, 0]) for ci in members(cg)]), 0.0)
        c_cat[g] = jnp.where(incl_cat, join([bot(gram[ci, pr, 0], gram[ci, pr, 1]) for ci in members(cg)]), 0.0)
        e_cat[g] = jnp.where(incl_cat, join([bot(gram[ci, pr, 1], gram[ci, pr, 0]) for ci in members(cg)]), 0.0)
    order_ac = (0, 1)
    order_be = (1, 0)

    n_pow = {g: -a_cat[g] for g in groups}
    t_inv = {g: eye_cat + n_pow[g] for g in groups}
    n_pow = {g: jnp.dot(n_pow[g].astype(BF16), block_diag(n_pow[g]), preferred_element_type=F32) for g in groups}
    for _ in range(L.bit_length() - 3):
        both = {g: jnp.dot(jnp.concatenate([n_pow[g], t_inv[g]], axis=0).astype(BF16), block_diag(n_pow[g]),
                           preferred_element_type=F32) for g in groups}
        n_pow = {g: both[g][0:L] for g in groups}
        t_inv = {g: t_inv[g] + both[g][L:2 * L] for g in groups}
    t_inv = {g: t_inv[g] + jnp.dot(t_inv[g].astype(BF16), block_diag(n_pow[g]), preferred_element_type=F32)
             for g in groups}

    def stacked(cg, order, part_fn):
        return jnp.concatenate([placed(ci, part_fn(ci, hh)) for ci in members(cg) for hh in order], axis=0)

    sec = lambda x, ci, width: x[:, (ci % gch) * width:(ci % gch + 1) * width]
    bv = {(pr, cg): jnp.dot(b_cat[pr, cg].astype(BF16),
                            stacked(cg, order_be, lambda ci, hh: [head_part(cut(v, ci, pr), hh)]),
                            preferred_element_type=F32) for pr, cg in groups}
    rows2 = lambda order, part_fn: jnp.concatenate(
        [jnp.concatenate(part_fn(hh), axis=1) for hh in order], axis=0)
    tr = {(ci, pr): jnp.dot(sec(t_inv[pr, ci // gch], ci, LANES).astype(BF16),
                            rows2(order_ac, lambda hh: [head_part(sec(bv[pr, ci // gch], ci, LANES), hh),
                                                        head_part(cut(kap_t, ci, pr), hh)]),
                            preferred_element_type=F32) for ci, pr in items}
    u0 = {it: -tr[it][:, 0:LANES] for it in items}
    wm = {it: tr[it][:, LANES:2 * LANES] for it in items}
    ce = {(ci, pr): jnp.dot(
        jnp.concatenate([sec(c_cat[pr, ci // gch], ci, LANES), sec(e_cat[pr, ci // gch], ci, LANES)],
                        axis=1).astype(BF16),
        jnp.concatenate([rows2(order_ac, lambda hh: [head_part(u0[ci, pr], hh), head_part(wm[ci, pr], hh)]),
                         rows2(order_be, lambda hh: [head_part(cut(v, ci, pr), hh), zeros_p])], axis=0),
        preferred_element_type=F32) for ci, pr in items}
    y0 = {it: ce[it][:, 0:LANES] for it in items}
    r_hat = {(ci, pr): cut(r_t, ci, pr) - ce[ci, pr][:, LANES:2 * LANES] for ci, pr in items}
    q_m = {it: -jnp.where(pair_diag, _mm_tn(wm[it], cut(beta_h, *it)), 0.0) for it in items}
    z_m = {it: jnp.where(pair_diag,
                         _mm_tn(jnp.concatenate([u0[it], cut(v, *it)], axis=0),
                                jnp.concatenate([cut(beta_h, *it), cut(k_h, *it)], axis=0)), 0.0)
           for it in items}

    s_mat = [s_scr[pr] for pr in range(npair)]
    y_rows = []
    for ci in range(nchunk):
        y_dep = [_mm_nt(r_hat[ci, pr], s_mat[pr]) for pr in range(npair)]
        s_q = [_mm(s_mat[pr], q_m[ci, pr]) for pr in range(npair)]
        s_mat = [s_mat[pr] * p_last[ci * L:ci * L + 1, pr * LANES:(pr + 1) * LANES] + s_q[pr] + z_m[ci, pr]
                 for pr in range(npair)]
        y_rows.append(jnp.concatenate([y_dep[pr] + y0[ci, pr] for pr in range(npair)], axis=-1))
    for pr in range(npair):
        s_scr[pr] = s_mat[pr]

    y = y_rows[0] if nchunk == 1 else jnp.concatenate(y_rows, axis=0)
    mu_y = _seg_sum(y, C_DH) * (1.0 / C_DH)
    yc = y - mu_y
    var = _seg_sum(yc * yc, C_DH) * (1.0 / C_DH)
    yn = yc * lax.rsqrt(var + RWKV_GN_EPS) * ln_w + ln_b
    bonus = _seg_sum(r * kc * r_k, C_DH) * v
    return (yn + bonus) * gate


def _mixer_body(lgs_ref, x_ref, gmix_ref, wa_ref, wb_ref, wc_ref,
                lbf_ref, oml_ref, gn_ref, ma0_ref,
                cos_ref, sin_ref, lg_ref, mb0_ref,
                sh0_ref, s0_ref, mu_ref, vec_ref, wl_ref,
                oa_ref, ob_ref, yc_ref, maout_ref, mbout_ref, sout_ref, shout_ref,
                ma_scr, mb_scr, d_scr, eq_scr, ev_scr, s_scr, sh_scr):
    c = pl.program_id(1)

    @pl.when(c == 0)
    def _():
        ma_scr[...] = ma0_ref[0]
        mb_scr[...] = mb0_ref[0]
        s_scr[...] = s0_ref[0]
        sh_scr[...] = sh0_ref[0]
        _ret_tables(lgs_ref, lg_ref, d_scr, eq_scr, ev_scr)

    hb = _rms(x_ref[0], gmix_ref[...]).astype(BF16)
    pa = jnp.dot(hb, wa_ref[...], preferred_element_type=F32)
    pb = jnp.dot(hb, wb_ref[...], preferred_element_type=F32)
    pc = jnp.dot(hb, wc_ref[...], preferred_element_type=F32)
    ob_ref[0] = _ret_tile(pb, cos_ref, sin_ref, lg_ref, mb_scr, d_scr, eq_scr, ev_scr)
    oa_ref[0] = _hgrn_tile(pa, lbf_ref, oml_ref, gn_ref, ma_scr)
    yc_ref[0] = _rwkv_tile(pc, mu_ref, vec_ref, wl_ref, s_scr, sh_scr)

    @pl.when(c == pl.num_programs(1) - 1)
    def _():
        maout_ref[0] = ma_scr[...]
        mbout_ref[0] = mb_scr[...]
        sout_ref[0] = s_scr[...]
        shout_ref[0] = sh_scr[...]


def _mixer(x, p, cos_t, sin_t, ma0, mb0, sh0, s0):
    bsz, t, d = x.shape
    tile = min(MIXER_TILE, t)
    npa, npc = A_HEADS // 2, C_HEADS // 2
    const = lambda b, c: (0, 0)
    tok = lambda w: pl.BlockSpec((1, tile, w), lambda b, c: (b, c, 0))
    full = lambda a: pl.BlockSpec(a.shape, const, pipeline_mode=pl.Buffered(1))
    pos = lambda w: pl.BlockSpec((tile, w), lambda b, c: (c, 0))
    per_batch = lambda *s: pl.BlockSpec((1,) + s, lambda b, c: (b,) + (0,) * len(s))
    ins = [(p['log_gamma'], pl.BlockSpec(memory_space=pltpu.SMEM)),
           (x, tok(d)), (p['nmix'], full(p['nmix'])),
           (p['wa'], full(p['wa'])), (p['wb'], full(p['wb'])), (p['wc'], full(p['wc'])),
           (p['lb_floor'], full(p['lb_floor'])), (p['one_minus_lb'], full(p['one_minus_lb'])),
           (p['gn'], full(p['gn'])), (ma0, per_batch(npa, LANES, LANES)),
           (cos_t, pos(2 * B_KW)), (sin_t, pos(2 * B_KW)), (p['lg_rows'], full(p['lg_rows'])),
           (mb0, per_batch(B_W, B_KW)),
           (sh0, per_batch(1, C_COLS)), (s0, per_batch(npc, LANES, LANES)),
           (p['mu'], full(p['mu'])), (p['vecs'], full(p['vecs'])),
           (p['w_lora'], full(p['w_lora']))]
    outs = [((bsz, t, A_W), tok(A_W)), ((bsz, t, B_W), tok(B_W)), ((bsz, t, C_W), tok(C_W)),
            ((bsz, npa, LANES, LANES), per_batch(npa, LANES, LANES)),
            ((bsz, B_W, B_KW), per_batch(B_W, B_KW)),
            ((bsz, npc, LANES, LANES), per_batch(npc, LANES, LANES)),
            ((bsz, 1, C_COLS), per_batch(1, C_COLS))]
    return pl.pallas_call(
        _mixer_body,
        grid=(bsz, t // tile),
        in_specs=[s for _, s in ins],
        out_specs=[s for _, s in outs],
        out_shape=[jax.ShapeDtypeStruct(shape, F32) for shape, _ in outs],
        scratch_shapes=[pltpu.VMEM((npa, LANES, LANES), F32), pltpu.VMEM((B_W, B_KW), F32),
                        pltpu.VMEM((B_HEADS, tile, tile), F32), pltpu.VMEM((tile, B_KW), F32),
                        pltpu.VMEM((tile, B_W), F32),
                        pltpu.VMEM((npc, LANES, LANES), F32), pltpu.VMEM((1, C_COLS), F32)],
        compiler_params=_cparams("parallel", "arbitrary"),
        name="mixer",
    )(*[a for a, _ in ins])


def _outattn_body(x_ref, oa_ref, ob_ref, yc_ref, woa_ref, wob_ref, woc_ref, g_ref, wq_ref, wo_ref,
                  mk_ref, mv_ref, o_ref):
    tm, d = x_ref.shape[1], x_ref.shape[2]
    hd = d // MEM_HEADS
    subs = _row_subtiles(tm)
    heads = [slice(h * hd, (h + 1) * hd) for h in range(MEM_HEADS)]
    mk = mk_ref[0].astype(BF16)
    mv = mv_ref[0].astype(BF16)
    x = [x_ref[0, r, :] + _mm(oa_ref[0, r, :], woa_ref[...]) + _mm(ob_ref[0, r, :], wob_ref[...])
         + _mm(yc_ref[0, r, :], woc_ref[...]) for r in subs]
    q = [_mm(_rms(xi, g_ref[...]), wq_ref[...]) for xi in x]
    s = [[_mm_nt(qi[:, c], mk[:, c]) * (hd ** -0.5) for c in heads] for qi in q]
    o = []
    for si in s:
        e = [jnp.exp(sh - jnp.max(sh, axis=-1, keepdims=True)) for sh in si]
        pr = [eh * (1.0 / jnp.sum(eh, axis=-1, keepdims=True)) for eh in e]
        o.append([_mm(ph, mv[:, c]) for ph, c in zip(pr, heads)])
    for r, xi, oi in zip(subs, x, o):
        o_ref[0, r, :] = xi + _mm(jnp.concatenate(oi, axis=-1), wo_ref[...])


def _outattn(x, oa, ob, yc, woa, wob, woc, g, wq, wo, mk, mv, layer):
    bsz, t, d = x.shape
    tm = min(ATTN_TILE, t)
    n_mem = mk.shape[2]
    const = lambda b, i: (0, 0)
    tok = lambda w: pl.BlockSpec((1, tm, w), lambda b, i: (b, i, 0))
    full = lambda a: pl.BlockSpec(a.shape, const, pipeline_mode=pl.Buffered(1))
    mem = pl.BlockSpec((None, 1, n_mem, d), lambda b, i: (layer, b, 0, 0))
    return pl.pallas_call(
        _outattn_body,
        grid=(bsz, t // tm),
        in_specs=[tok(d), tok(A_W), tok(B_W), tok(C_W), full(woa), full(wob), full(woc), full(g),
                  full(wq), full(wo), mem, mem],
        out_specs=tok(d),
        out_shape=jax.ShapeDtypeStruct((bsz, t, d), F32),
        compiler_params=_cparams("parallel", "parallel"),
        name="outattn",
    )(x, oa, ob, yc, woa, wob, woc, g, wq, wo, mk, mv)


def _blockdiag_t(s):
    bsz, h, dk, dv = s.shape
    eye = jnp.eye(h, dtype=s.dtype)
    return jnp.einsum('bhkv,hg->bhvgk', s, eye).reshape(bsz, h * dv, h * dk)


def _unblockdiag_t(m, h, dk, dv):
    bsz = m.shape[0]
    m5 = m.reshape(bsz, h, dv, h, dk)
    return jnp.stack([m5[:, i, :, i, :] for i in range(h)], axis=1).transpose(0, 1, 3, 2)


def _pair_pack(s):
    bsz, h, d, _ = s.shape
    s4 = s.reshape(bsz, h // 2, 2, d, d)
    eye = jnp.eye(2, dtype=s.dtype)
    return jnp.einsum('bphvk,hg->bphvgk', s4, eye).reshape(bsz, h // 2, 2 * d, 2 * d)


def _pair_unpack(m):
    bsz, npair, w, _ = m.shape
    d = w // 2
    m6 = m.reshape(bsz, npair, 2, d, 2, d)
    return jnp.stack([m6[:, :, i, :, i, :] for i in range(2)], axis=2).reshape(bsz, 2 * npair, d, d)


def _rope_tables(pos):
    half = B_DK // 2
    inv = ROPE_BASE ** (-jnp.arange(half, dtype=F32) / half)
    ang = pos.astype(F32)[:, None] * inv[None, :]
    cos = jnp.tile(jnp.cos(ang), (1, 2 * 2 * B_HEADS))
    sin = jnp.tile(jnp.sin(ang), (1, 2 * 2 * B_HEADS))
    return cos, sin


def _prep_layer(l, W, lbs, log_gamma):
    row = lambda a: a.reshape(1, -1).astype(F32)
    w_in = W['w_in'][l]
    pad_rows = lambda m, off: jnp.zeros((C_LORA, C_W), F32).at[off:off + m.shape[0]].set(m).astype(BF16)
    lb = lbs[l]
    zero = jnp.zeros((C_W,), F32)
    vecs = jnp.stack([W['rwkv_w0'][l], W['rwkv_a0'][l], W['rwkv_k_k'][l], W['rwkv_k_a'][l],
                      W['rwkv_r_k'][l].reshape(-1), W['rwkv_ln_w'][l], W['rwkv_ln_b'][l], zero])
    lg_rows = jnp.stack([jnp.pad(jnp.repeat(log_gamma, B_DK), (0, B_W - B_KW)),
                         jnp.repeat(log_gamma, B_DV)])
    w_out = W['w_out'][l].astype(BF16)
    return dict(
        n1=row(W['norm_ffn1'][l]), g1=W['ffn1_wg'][l].astype(BF16), u1=W['ffn1_wu'][l].astype(BF16),
        d1=W['ffn1_wd'][l].astype(BF16),
        n2=row(W['norm_ffn2'][l]), g2=W['ffn2_wg'][l].astype(BF16), u2=W['ffn2_wu'][l].astype(BF16),
        d2=W['ffn2_wd'][l].astype(BF16),
        nmix=row(W['norm_mix'][l]),
        wa=w_in[:, :A_COLS].astype(BF16), wb=w_in[:, A_COLS:A_COLS + B_COLS].astype(BF16),
        wc=w_in[:, A_COLS + B_COLS:].astype(BF16),
        lb_floor=row(jnp.maximum(lb, LB_FLOOR)), one_minus_lb=row(1.0 - lb),
        gn=row(jnp.tile(W['hgrn_norm'][l], A_HEADS)),
        lg_rows=lg_rows, log_gamma=log_gamma,
        mu=row(W['rwkv_mu'][l]), vecs=vecs,
        w_lora=jnp.concatenate([pad_rows(W['rwkv_w2'][l], 0), pad_rows(W['rwkv_a2'][l], C_DECAY_RANK),
                                pad_rows(W['rwkv_g2'][l], C_DECAY_RANK + C_AAA_RANK)], axis=1),
        woa=w_out[:A_W], wob=w_out[A_W:A_W + B_W], woc=w_out[A_W + B_W:],
        ncross=row(W['norm_cross'][l]), wq=W['wq_x'][l].astype(BF16), wo=W['wo_x'][l].astype(BF16),
    )


def _trunk(x, pos, mem_k, mem_v, st_a, st_b, st_c, st_sh, layers, final_g):
    bsz, t, d = x.shape
    n = bsz * t
    depth = len(layers)
    cos_t, sin_t = _rope_tables(pos)
    new_a, new_b, new_c, new_sh = [], [], [], []
    x2 = x.reshape(n, d)
    for l, p in enumerate(layers):
        x2 = _ffn(x2, p['n1'], p['g1'], p['u1'], p['d1'], final_g, False)
        oa, ob, yc, ma, mb, sc, sh = _mixer(
            x2.reshape(bsz, t, d), p, cos_t, sin_t, _pair_pack(st_a[l].transpose(0, 1, 3, 2)),
            _blockdiag_t(st_b[l]), st_sh[l], _pair_pack(st_c[l]))
        x3 = _outattn(x2.reshape(bsz, t, d), oa, ob, yc, p['woa'], p['wob'], p['woc'], p['ncross'],
                      p['wq'], p['wo'], mem_k.reshape(depth, bsz, -1, d), mem_v.reshape(depth, bsz, -1, d), l)
        x2 = _ffn(x3.reshape(n, d), p['n2'], p['g2'], p['u2'], p['d2'], final_g, l == depth - 1)
        new_a.append(_pair_unpack(ma).transpose(0, 1, 3, 2))
        new_b.append(_unblockdiag_t(mb, B_HEADS, B_DK, B_DV))
        new_c.append(_pair_unpack(sc))
        new_sh.append(sh)
    return (x2.reshape(bsz, t, d), jnp.stack(new_a), jnp.stack(new_b), jnp.stack(new_c),
            jnp.stack(new_sh))


def kernel(x_prompt, x_sample, mem_prompt, cache_mem_k, cache_mem_v, state_hgrn, state_ret, state_rwkv, state_rwkv_shift, norm_ffn1, ffn1_wg, ffn1_wu, ffn1_wd, norm_mix, w_in, hgrn_lb_param, hgrn_norm, rwkv_mu, rwkv_w0, rwkv_w2, rwkv_a0, rwkv_a2, rwkv_g2, rwkv_k_k, rwkv_k_a, rwkv_r_k, rwkv_ln_w, rwkv_ln_b, w_out, norm_cross, norm_mem, wq_x, wk_x, wv_x, wo_x, norm_ffn2, ffn2_wg, ffn2_wu, ffn2_wd, final_norm):
    W = dict(norm_ffn1=norm_ffn1, ffn1_wg=ffn1_wg, ffn1_wu=ffn1_wu, ffn1_wd=ffn1_wd, norm_mix=norm_mix,
             w_in=w_in, hgrn_norm=hgrn_norm, rwkv_mu=rwkv_mu, rwkv_w0=rwkv_w0, rwkv_w2=rwkv_w2,
             rwkv_a0=rwkv_a0, rwkv_a2=rwkv_a2, rwkv_g2=rwkv_g2, rwkv_k_k=rwkv_k_k, rwkv_k_a=rwkv_k_a,
             rwkv_r_k=rwkv_r_k, rwkv_ln_w=rwkv_ln_w, rwkv_ln_b=rwkv_ln_b, w_out=w_out,
             norm_cross=norm_cross, wq_x=wq_x, wo_x=wo_x, norm_ffn2=norm_ffn2, ffn2_wg=ffn2_wg,
             ffn2_wu=ffn2_wu, ffn2_wd=ffn2_wd)
    depth = w_in.shape[0]
    d = x_prompt.shape[-1]
    sm = jax.nn.softmax(hgrn_lb_param.astype(F32), axis=0)
    lbs = jnp.cumsum(sm, axis=0) - sm[0:1]
    log_gamma = jnp.log1p(-jnp.exp2(-5.0 - jnp.arange(B_HEADS, dtype=F32)))
    layers = [_prep_layer(l, W, lbs, log_gamma) for l in range(depth)]
    final_g = final_norm.reshape(1, -1).astype(F32)

    bp, tp, _ = x_prompt.shape
    n_mem = mem_prompt.shape[1]
    hd = d // MEM_HEADS
    mem2 = mem_prompt.reshape(bp * n_mem, d)
    mks, mvs = [], []
    for l in range(depth):
        mk, mv = _memkv(mem2, norm_mem[l].reshape(1, -1), wk_x[l].astype(BF16), wv_x[l].astype(BF16))
        mks.append(mk)
        mvs.append(mv)
    mem_k_all = jnp.stack(mks).reshape(depth, bp, n_mem, d)
    mem_v_all = jnp.stack(mvs).reshape(depth, bp, n_mem, d)
    p_mem_k = mem_k_all.reshape(depth, bp, n_mem, MEM_HEADS, hd)
    p_mem_v = mem_v_all.reshape(depth, bp, n_mem, MEM_HEADS, hd)

    zeros = lambda *s: jnp.zeros((depth, bp) + s, F32)
    y_prompt, p_hgrn, p_ret, p_rwkv, p_shift = _trunk(
        x_prompt, jnp.arange(tp, dtype=jnp.int32), mem_k_all, mem_v_all,
        zeros(A_HEADS, A_DK, A_DV), zeros(B_HEADS, B_DK, B_DV), zeros(C_HEADS, C_DH, C_DH),
        zeros(1, C_COLS), layers, final_g)

    ts = x_sample.shape[1]
    y_sample, s_hgrn, s_ret, s_rwkv, s_shift = _trunk(
        x_sample, PAST_LEN + jnp.arange(ts, dtype=jnp.int32), cache_mem_k, cache_mem_v,
        state_hgrn, state_ret, state_rwkv, state_rwkv_shift, layers, final_g)

    return (y_prompt, y_sample, p_hgrn, p_ret, p_rwkv, p_shift, p_mem_k, p_mem_v,
            s_hgrn, s_ret, s_rwkv, s_shift)
```
